```python
import jax, jax.numpy as jnp
from jax import lax
import numpy as np

D_MODEL = 1024
BATCH = 8
SEQ = 8192
DEPTH = 4
DEC_BATCH = 16
DEC_SEQ = 16
PAST_LEN = 1024

CHUNK = 64
CONV_WIDTH = 4
EPS = 1e-5
D_MIX = 2 * D_MODEL
D_GROUP = D_MIX // 4
SSD_HEAD_DIM = 64
SSD_HEADS = D_GROUP // SSD_HEAD_DIM
SSD_GROUPS = 2
SSD_STATE = 64
SSD_CONV_DIM = D_GROUP + 2 * SSD_GROUPS * SSD_STATE
MLSTM_HEADS = 4
MLSTM_DK = D_GROUP // MLSTM_HEADS
MLSTM_DV = D_GROUP // MLSTM_HEADS
RGLRU_BLOCKS = 8
RGLRU_BLOCK_DIM = D_GROUP // RGLRU_BLOCKS
RGLRU_C = 8.0
GLA_HEADS = 4
GLA_DV = D_GROUP // GLA_HEADS
GLA_DK = GLA_DV // 2
GLA_GATE_RANK = 16
GLA_TAU = 16.0
D_FF = 4 * D_MODEL
DEEPNORM_ALPHA = (2 * DEPTH) ** 0.25
DEEPNORM_BETA = (8 * DEPTH) ** -0.25

IN_SPLIT_SIZES = (
    D_GROUP,
    SSD_CONV_DIM,
    SSD_HEADS,
    MLSTM_HEADS * MLSTM_DK,
    MLSTM_HEADS * MLSTM_DK,
    MLSTM_HEADS * MLSTM_DV,
    2 * MLSTM_HEADS,
    MLSTM_HEADS * MLSTM_DV,
    D_GROUP,
    D_GROUP,
    GLA_HEADS * GLA_DK,
    GLA_HEADS * GLA_DK,
    GLA_HEADS * GLA_DV,
    GLA_GATE_RANK,
    GLA_HEADS * GLA_DV,
)
D_IN = sum(IN_SPLIT_SIZES)
IN_SPLIT_POINTS = tuple(int(s) for s in np.cumsum(IN_SPLIT_SIZES)[:-1])

kernel_name = "hybrid_ssd_mlstm_rglru_gla_stream_step"


def layer_norm(x, g, b):
    xf = x.astype(jnp.float32)
    mu = jnp.mean(xf, -1, keepdims=True)
    var = jnp.mean(jnp.square(xf - mu), -1, keepdims=True)
    return ((xf - mu) * lax.rsqrt(var + EPS) * g + b).astype(x.dtype)


def rms_norm(x, g):
    return x * lax.rsqrt(jnp.mean(x * x, -1, keepdims=True) + EPS) * g


def head_rms_norm(t, g):
    t = t * lax.rsqrt(jnp.mean(t * t, -1, keepdims=True) + EPS)
    return t.reshape(t.shape[:2] + (-1,)) * g


def causal_conv(u, buf, w, b):
    seq = u.shape[1]
    up = jnp.concatenate([buf, u], axis=1)
    out = b + up[:, 0:seq] * w[0]
    for j in range(1, CONV_WIDTH):
        out = out + up[:, j:j + seq] * w[j]
    return out, up[:, seq:]


def _chunk_len(seq):
    return CHUNK if seq % CHUNK == 0 else seq


def _chunks(t, nc, cs):
    return t.reshape((t.shape[0], nc, cs) + t.shape[2:])


def ssd_scan(xs, bm, cm, dt, a_neg, d_skip, h0):
    bsz, seq, nh, _ = xs.shape
    cs = _chunk_len(seq)
    nc = seq // cs
    rep = nh // SSD_GROUPS
    bh = jnp.repeat(bm, rep, axis=2)
    ch = jnp.repeat(cm, rep, axis=2)
    xc, bc, cc, dtc = (_chunks(t, nc, cs) for t in (xs, bh, ch, dt))
    acum = jnp.cumsum(dtc * a_neg, axis=2)
    tri = jnp.tril(jnp.ones((cs, cs), dtype=bool))[:, :, None]
    decay = jnp.exp(jnp.where(tri, acum[:, :, :, None, :] - acum[:, :, None, :, :], -jnp.inf))
    scores = jnp.einsum('bcthn,bcshn->bctsh', cc, bc) * decay * dtc[:, :, None, :, :]
    y_intra = jnp.einsum('bctsh,bcshp->bcthp', scores, xc)
    w_last = jnp.exp(acum[:, :, -1:, :] - acum) * dtc
    s_chunk = jnp.einsum('bcsh,bcshp,bcshn->bchpn', w_last, xc, bc)
    d_chunk = jnp.exp(acum[:, :, -1, :])

    def step(h, inp):
        s_c, d_c = inp
        return d_c[:, :, None, None] * h + s_c, h

    h_last, h_in = lax.scan(step, h0, (jnp.moveaxis(s_chunk, 1, 0), jnp.moveaxis(d_chunk, 1, 0)))
    h_in = jnp.moveaxis(h_in, 0, 1)
    y_inter = jnp.einsum('bcthn,bchpn->bcthp', cc * jnp.exp(acum)[..., None], h_in)
    y = (y_intra + y_inter).reshape(xs.shape) + d_skip[:, None] * xs
    return y, h_last


def mlstm_scan(q, k, v, i_pre, f_pre, c0, n0, m0):
    bsz, seq, nh, dk = q.shape
    cs = _chunk_len(seq)
    nc = seq // cs
    k = k * (dk ** -0.5)
    logf = jax.nn.log_sigmoid(f_pre)
    qc, kc, vc, ic, lfc = (_chunks(t, nc, cs) for t in (q, k, v, i_pre, logf))
    bcum = jnp.cumsum(lfc, axis=2)
    tri = jnp.tril(jnp.ones((cs, cs), dtype=bool))[:, :, None]
    dmat = jnp.where(tri, bcum[:, :, :, None, :] - bcum[:, :, None, :, :] + ic[:, :, None, :, :], -jnp.inf)
    m_intra = jnp.max(dmat, axis=3)
    qk = jnp.einsum('bcthd,bcshd->bctsh', qc, kc) * jnp.exp(dmat - m_intra[:, :, :, None, :])
    num_intra = jnp.einsum('bctsh,bcshv->bcthv', qk, vc)
    den_intra = jnp.sum(qk, axis=3)
    b_last = bcum[:, :, -1, :]
    lw = b_last[:, :, None, :] - bcum + ic
    m_loc = jnp.max(lw, axis=2)
    wl = jnp.exp(lw - m_loc[:, :, None, :])
    c_chunk = jnp.einsum('bcsh,bcshd,bcshv->bchdv', wl, kc, vc)
    n_chunk = jnp.einsum('bcsh,bcshd->bchd', wl, kc)

    def step(carry, inp):
        c_s, n_s, m_s = carry
        c_c, n_c, m_c, bl = inp
        m_new = jnp.maximum(bl + m_s, m_c)
        a_old = jnp.exp(bl + m_s - m_new)
        a_new = jnp.exp(m_c - m_new)
        c_out = a_old[..., None, None] * c_s + a_new[..., None, None] * c_c
        n_out = a_old[..., None] * n_s + a_new[..., None] * n_c
        return (c_out, n_out, m_new), (c_s, n_s, m_s)

    xs_in = tuple(jnp.moveaxis(t, 1, 0) for t in (c_chunk, n_chunk, m_loc, b_last))
    (c_last, n_last, m_last), (c_in, n_in, m_in) = lax.scan(step, (c0, n0, m0), xs_in)
    c_in = jnp.moveaxis(c_in, 0, 1)
    n_in = jnp.moveaxis(n_in, 0, 1)
    m_in = jnp.moveaxis(m_in, 0, 1)
    g = bcum + m_in[:, :, None, :]
    m_t = jnp.maximum(g, m_intra)
    a_inter = jnp.exp(g - m_t)
    a_intra = jnp.exp(m_intra - m_t)
    num = a_inter[..., None] * jnp.einsum('bcthd,bchdv->bcthv', qc, c_in) + a_intra[..., None] * num_intra
    den = a_inter * jnp.einsum('bcthd,bchd->bcth', qc, n_in) + a_intra * den_intra
    h = num / jnp.maximum(jnp.abs(den), jnp.exp(-m_t))[..., None]
    return h.reshape(v.shape), c_last, n_last, m_last


def rglru_scan(xr, w_a, b_a, w_x, b_x, lam, h0):
    bsz, seq, dim = xr.shape
    xb = xr.reshape(bsz, seq, RGLRU_BLOCKS, RGLRU_BLOCK_DIM)
    r = jax.nn.sigmoid(jnp.einsum('blnk,nkj->blnj', xb, w_a).reshape(bsz, seq, dim) + b_a)
    i = jax.nn.sigmoid(jnp.einsum('blnk,nkj->blnj', xb, w_x).reshape(bsz, seq, dim) + b_x)
    log_a = RGLRU_C * r * jax.nn.log_sigmoid(lam)
    a = jnp.exp(log_a)
    bterm = jnp.sqrt(-jnp.expm1(2.0 * log_a)) * (i * xr)
    bterm = bterm.at[:, 0].add(a[:, 0] * h0)

    def combine(left, right):
        a1, b1 = left
        a2, b2 = right
        return a1 * a2, a2 * b1 + b2

    _, h = lax.associative_scan(combine, (a, bterm), axis=1)
    return h, h[:, -1]


def gla_scan(q, k, v, log_alpha, s0):
    bsz, seq, nh, dk = q.shape
    cs = _chunk_len(seq)
    nc = seq // cs
    q = q * (dk ** -0.5)
    qc, kc, vc, lac = (_chunks(t, nc, cs) for t in (q, k, v, log_alpha))
    bcum = jnp.cumsum(lac, axis=2)
    q_t = qc * jnp.exp(bcum)
    k_t = kc * jnp.exp(-bcum)
    tri = jnp.tril(jnp.ones((cs, cs), dtype=bool))[:, :, None]
    att = jnp.where(tri, jnp.einsum('bcthd,bcshd->bctsh', q_t, k_t), 0.0)
    o_intra = jnp.einsum('bctsh,bcshv->bcthv', att, vc)
    b_last = bcum[:, :, -1]
    k_d = kc * jnp.exp(b_last[:, :, None] - bcum)
    s_chunk = jnp.einsum('bcshd,bcshv->bchdv', k_d, vc)
    d_chunk = jnp.exp(b_last)

    def step(s, inp):
        s_c, d_c = inp
        return d_c[..., None] * s + s_c, s

    s_last, s_in = lax.scan(step, s0, (jnp.moveaxis(s_chunk, 1, 0), jnp.moveaxis(d_chunk, 1, 0)))
    s_in = jnp.moveaxis(s_in, 0, 1)
    o_inter = jnp.einsum('bcthd,bchdv->bcthv', q_t, s_in)
    return (o_intra + o_inter).reshape(v.shape), s_last


def state_shapes(bsz):
    return (
        (bsz, SSD_HEADS, SSD_HEAD_DIM, SSD_STATE),
        (bsz, CONV_WIDTH - 1, SSD_CONV_DIM),
        (bsz, MLSTM_HEADS, MLSTM_DK, MLSTM_DV),
        (bsz, MLSTM_HEADS, MLSTM_DK),
        (bsz, MLSTM_HEADS),
        (bsz, D_GROUP),
        (bsz, CONV_WIDTH - 1, D_GROUP),
        (bsz, GLA_HEADS, GLA_DK, GLA_DV),
    )


def trunk_layer(x, states, p):
    f32 = jnp.float32
    bsz, seq, _ = x.shape
    s_ssd_h, s_ssd_conv, s_c, s_n, s_m, s_rg_h, s_rg_conv, s_gla = (s.astype(f32) for s in states)
    proj = jnp.einsum('bld,de->ble', x, p['w_in']).astype(f32)
    (z_s, xbc_s, dt_s, q_m, k_m, v_m, if_m, o_m,
     x_r, y_r, q_g, k_g, v_g, a_g, g_g) = jnp.split(proj, IN_SPLIT_POINTS, axis=-1)

    xbc, new_ssd_conv = causal_conv(xbc_s, s_ssd_conv, p['ssd_conv_w'], p['ssd_conv_b'])
    xbc = jax.nn.silu(xbc)
    xs, bm, cm = jnp.split(xbc, (D_GROUP, D_GROUP + SSD_GROUPS * SSD_STATE), axis=-1)
    dt = jax.nn.softplus(dt_s + p['ssd_dt_bias'])
    a_neg = -jnp.exp(p['ssd_A_log'].astype(f32))
    y_ssd, new_ssd_h = ssd_scan(
        xs.reshape(bsz, seq, SSD_HEADS, SSD_HEAD_DIM),
        bm.reshape(bsz, seq, SSD_GROUPS, SSD_STATE),
        cm.reshape(bsz, seq, SSD_GROUPS, SSD_STATE),
        dt, a_neg, p['ssd_D'], s_ssd_h)
    y_ssd = rms_norm(y_ssd.reshape(bsz, seq, D_GROUP) * jax.nn.silu(z_s), p['ssd_norm_w'])

    ifg = if_m + p['mlstm_if_b']
    h_m, new_c, new_n, new_m = mlstm_scan(
        q_m.reshape(bsz, seq, MLSTM_HEADS, MLSTM_DK),
        k_m.reshape(bsz, seq, MLSTM_HEADS, MLSTM_DK),
        v_m.reshape(bsz, seq, MLSTM_HEADS, MLSTM_DV),
        ifg[..., :MLSTM_HEADS], ifg[..., MLSTM_HEADS:], s_c, s_n, s_m)
    y_m = jax.nn.sigmoid(o_m) * head_rms_norm(h_m, p['mlstm_norm_w'])

    xr, new_rg_conv = causal_conv(x_r, s_rg_conv, p['rg_conv_w'], p['rg_conv_b'])
    h_r, new_rg_h = rglru_scan(xr, p['rg_gate_a_w'], p['rg_gate_a_b'], p['rg_gate_x_w'],
                               p['rg_gate_x_b'], p['rg_lambda'], s_rg_h)
    y_rg = h_r * jax.nn.gelu(y_r)

    log_alpha = jax.nn.log_sigmoid(a_g @ p['gla_gate_w2'] + p['gla_gate_b']) / GLA_TAU
    o_g, new_gla = gla_scan(
        q_g.reshape(bsz, seq, GLA_HEADS, GLA_DK),
        k_g.reshape(bsz, seq, GLA_HEADS, GLA_DK),
        v_g.reshape(bsz, seq, GLA_HEADS, GLA_DV),
        log_alpha.reshape(bsz, seq, GLA_HEADS, GLA_DK), s_gla)
    y_g = head_rms_norm(o_g, p['gla_norm_w']) * jax.nn.silu(g_g)

    mix = jnp.concatenate([y_ssd, y_m, y_rg, y_g], axis=-1).astype(x.dtype)
    x = layer_norm(DEEPNORM_ALPHA * x + mix @ p['w_out'], p['ln1_g'], p['ln1_b'])
    hid = jnp.square(jax.nn.relu(x @ p['mlp_w1'] + p['mlp_b1']))
    x = layer_norm(DEEPNORM_ALPHA * x + hid @ p['mlp_w2'] + p['mlp_b2'], p['ln2_g'], p['ln2_b'])
    new_states = tuple(s.astype(x.dtype) for s in
                       (new_ssd_h, new_ssd_conv, new_c, new_n, new_m, new_rg_h, new_rg_conv, new_gla))
    return x, new_states


def setup_inputs(seed: int = 0) -> dict:
    key = jax.random.key(seed)
    ks = iter(jax.random.split(key, 64))

    def nrm(shape, scale=1.0):
        return scale * jax.random.normal(next(ks), shape, jnp.float32)

    def unif(shape, lo, hi):
        return jax.random.uniform(next(ks), shape, jnp.float32, lo, hi)

    L = DEPTH
    sh = state_shapes(DEC_BATCH)
    dt0 = jnp.exp(unif((L, SSD_HEADS), float(np.log(1e-3)), float(np.log(1e-1))))
    a_pow = unif((L, D_GROUP), 0.9, 0.999)
    s_base = a_pow ** (1.0 / RGLRU_C)
    f_bias = jnp.linspace(3.0, 6.0, MLSTM_HEADS, dtype=jnp.float32)[None, :] + nrm((L, MLSTM_HEADS), 0.1)
    i_bias = nrm((L, MLSTM_HEADS), 0.1)
    return {
        'x_prompt': nrm((BATCH, SEQ, D_MODEL)),
        'x_sample': nrm((DEC_BATCH, DEC_SEQ, D_MODEL)),
        'state_ssd_h': nrm((L,) + sh[0], 0.1),
        'state_ssd_conv': nrm((L,) + sh[1]),
        'state_mlstm_C': nrm((L,) + sh[2], 0.1),
        'state_mlstm_n': nrm((L,) + sh[3], 0.1),
        'state_mlstm_m': nrm((L,) + sh[4], 0.5),
        'state_rglru_h': nrm((L,) + sh[5], 0.5),
        'state_rglru_conv': nrm((L,) + sh[6]),
        'state_gla_S': nrm((L,) + sh[7], 0.1),
        'ln_in_g': 1.0 + nrm((D_MODEL,), 0.02),
        'ln_in_b': nrm((D_MODEL,), 0.02),
        'w_in': nrm((L, D_MODEL, D_IN), D_MODEL ** -0.5),
        'ssd_conv_w': nrm((L, CONV_WIDTH, SSD_CONV_DIM), CONV_WIDTH ** -0.5),
        'ssd_conv_b': nrm((L, SSD_CONV_DIM), 0.02),
        'ssd_dt_bias': dt0 + jnp.log(-jnp.expm1(-dt0)),
        'ssd_A_log': jnp.log(unif((L, SSD_HEADS), 1.0, 16.0)),
        'ssd_D': 1.0 + nrm((L, SSD_HEADS), 0.1),
        'ssd_norm_w': 1.0 + nrm((L, D_GROUP), 0.02),
        'mlstm_if_b': jnp.concatenate([i_bias, f_bias], axis=-1),
        'mlstm_norm_w': 1.0 + nrm((L, D_GROUP), 0.02),
        'rg_conv_w': nrm((L, CONV_WIDTH, D_GROUP), CONV_WIDTH ** -0.5),
        'rg_conv_b': nrm((L, D_GROUP), 0.02),
        'rg_gate_a_w': nrm((L, RGLRU_BLOCKS, RGLRU_BLOCK_DIM, RGLRU_BLOCK_DIM), RGLRU_BLOCK_DIM ** -0.5),
        'rg_gate_a_b': nrm((L, D_GROUP), 0.02),
        'rg_gate_x_w': nrm((L, RGLRU_BLOCKS, RGLRU_BLOCK_DIM, RGLRU_BLOCK_DIM), RGLRU_BLOCK_DIM ** -0.5),
        'rg_gate_x_b': nrm((L, D_GROUP), 0.02),
        'rg_lambda': jnp.log(s_base) - jnp.log1p(-s_base),
        'gla_gate_w2': nrm((L, GLA_GATE_RANK, GLA_HEADS * GLA_DK), GLA_GATE_RANK ** -0.5),
        'gla_gate_b': nrm((L, GLA_HEADS * GLA_DK), 0.1),
        'gla_norm_w': 1.0 + nrm((L, D_GROUP), 0.02),
        'w_out': nrm((L, D_MIX, D_MODEL), DEEPNORM_BETA * D_MIX ** -0.5),
        'ln1_g': 1.0 + nrm((L, D_MODEL), 0.02),
        'ln1_b': nrm((L, D_MODEL), 0.02),
        'mlp_w1': nrm((L, D_MODEL, D_FF), D_MODEL ** -0.5),
        'mlp_b1': nrm((L, D_FF), 0.02),
        'mlp_w2': nrm((L, D_FF, D_MODEL), DEEPNORM_BETA * D_FF ** -0.5),
        'mlp_b2': nrm((L, D_MODEL), 0.02),
        'ln2_g': 1.0 + nrm((L, D_MODEL), 0.02),
        'ln2_b': nrm((L, D_MODEL), 0.02),
    }


def reference(x_prompt, x_sample, state_ssd_h, state_ssd_conv, state_mlstm_C, state_mlstm_n,
              state_mlstm_m, state_rglru_h, state_rglru_conv, state_gla_S, ln_in_g, ln_in_b,
              w_in, ssd_conv_w, ssd_conv_b, ssd_dt_bias, ssd_A_log, ssd_D, ssd_norm_w,
              mlstm_if_b, mlstm_norm_w, rg_conv_w, rg_conv_b, rg_gate_a_w, rg_gate_a_b,
              rg_gate_x_w, rg_gate_x_b, rg_lambda, gla_gate_w2, gla_gate_b, gla_norm_w,
              w_out, ln1_g, ln1_b, mlp_w1, mlp_b1, mlp_w2, mlp_b2, ln2_g, ln2_b):
    xp = layer_norm(x_prompt, ln_in_g, ln_in_b)
    xs = layer_norm(x_sample, ln_in_g, ln_in_b)
    prompt_states = []
    sample_states = []
    for l in range(DEPTH):
        p = dict(
            w_in=w_in[l], ssd_conv_w=ssd_conv_w[l], ssd_conv_b=ssd_conv_b[l],
            ssd_dt_bias=ssd_dt_bias[l], ssd_A_log=ssd_A_log[l], ssd_D=ssd_D[l],
            ssd_norm_w=ssd_norm_w[l], mlstm_if_b=mlstm_if_b[l], mlstm_norm_w=mlstm_norm_w[l],
            rg_conv_w=rg_conv_w[l], rg_conv_b=rg_conv_b[l], rg_gate_a_w=rg_gate_a_w[l],
            rg_gate_a_b=rg_gate_a_b[l], rg_gate_x_w=rg_gate_x_w[l], rg_gate_x_b=rg_gate_x_b[l],
            rg_lambda=rg_lambda[l], gla_gate_w2=gla_gate_w2[l], gla_gate_b=gla_gate_b[l],
            gla_norm_w=gla_norm_w[l], w_out=w_out[l], ln1_g=ln1_g[l], ln1_b=ln1_b[l],
            mlp_w1=mlp_w1[l], mlp_b1=mlp_b1[l], mlp_w2=mlp_w2[l], mlp_b2=mlp_b2[l],
            ln2_g=ln2_g[l], ln2_b=ln2_b[l])
        zero_states = tuple(jnp.zeros(s, xp.dtype) for s in state_shapes(xp.shape[0]))
        xp, st_p = trunk_layer(xp, zero_states, p)
        cached = (state_ssd_h[l], state_ssd_conv[l], state_mlstm_C[l], state_mlstm_n[l],
                  state_mlstm_m[l], state_rglru_h[l], state_rglru_conv[l], state_gla_S[l])
        xs, st_s = trunk_layer(xs, cached, p)
        prompt_states.append(st_p)
        sample_states.append(st_s)
    (p_ssd_h, p_ssd_conv, p_mlstm_C, p_mlstm_n, p_mlstm_m, p_rglru_h, p_rglru_conv,
     p_gla_S) = [jnp.stack(v, axis=0) for v in zip(*prompt_states)]
    (s_ssd_h, s_ssd_conv, s_mlstm_C, s_mlstm_n, s_mlstm_m, s_rglru_h, s_rglru_conv,
     s_gla_S) = [jnp.stack(v, axis=0) for v in zip(*sample_states)]
    return (xp, xs,
            p_ssd_h, p_ssd_conv, p_mlstm_C, p_mlstm_n, p_mlstm_m, p_rglru_h, p_rglru_conv, p_gla_S,
            s_ssd_h, s_ssd_conv, s_mlstm_C, s_mlstm_n, s_mlstm_m, s_rglru_h, s_rglru_conv, s_gla_S)
```

```python
import functools

import jax
import jax.numpy as jnp
from jax import lax
from jax.experimental import pallas as pl
from jax.experimental.pallas import tpu as pltpu

F32 = jnp.float32
BF16 = jnp.bfloat16

D_MODEL = 1024
DEPTH = 4
CHUNK = 64
CONV_WIDTH = 4
EPS = 1e-5
D_MIX = 2 * D_MODEL
D_GROUP = D_MIX // 4
SSD_HEAD_DIM = 64
SSD_HEADS = D_GROUP // SSD_HEAD_DIM
SSD_GROUPS = 2
SSD_STATE = 64
SSD_CONV_DIM = D_GROUP + 2 * SSD_GROUPS * SSD_STATE
MLSTM_HEADS = 4
MLSTM_DK = D_GROUP // MLSTM_HEADS
MLSTM_DV = D_GROUP // MLSTM_HEADS
RGLRU_BLOCKS = 8
RGLRU_BLOCK_DIM = D_GROUP // RGLRU_BLOCKS
RGLRU_C = 8.0
GLA_HEADS = 4
GLA_DV = D_GROUP // GLA_HEADS
GLA_DK = GLA_DV // 2
GLA_GATE_RANK = 16
GLA_TAU = 16.0
D_FF = 4 * D_MODEL
DEEPNORM_ALPHA = (2 * DEPTH) ** 0.25

IN_SPLIT_SIZES = (
    D_GROUP, SSD_CONV_DIM, SSD_HEADS,
    MLSTM_HEADS * MLSTM_DK, MLSTM_HEADS * MLSTM_DK, MLSTM_HEADS * MLSTM_DV, 2 * MLSTM_HEADS,
    MLSTM_HEADS * MLSTM_DV, D_GROUP, D_GROUP,
    GLA_HEADS * GLA_DK, GLA_HEADS * GLA_DK, GLA_HEADS * GLA_DV, GLA_GATE_RANK, GLA_HEADS * GLA_DV,
)
_IN_OFFS = [0]
for _s in IN_SPLIT_SIZES:
    _IN_OFFS.append(_IN_OFFS[-1] + _s)

LANES = 128
SUBLANES = 8
VMEM_LIMIT_BYTES = 56 * 1024 * 1024

OFF_Z = 0
OFF_XBC = OFF_Z + D_GROUP
OFF_QM = OFF_XBC + SSD_CONV_DIM
OFF_KM = OFF_QM + D_GROUP
OFF_VM = OFF_KM + D_GROUP
OFF_OM = OFF_VM + D_GROUP
OFF_XR = OFF_OM + D_GROUP
OFF_YR = OFF_XR + D_GROUP
OFF_QG = OFF_YR + D_GROUP
OFF_KG = OFF_QG + GLA_HEADS * GLA_DK
OFF_VG = OFF_KG + GLA_HEADS * GLA_DK
OFF_GG = OFF_VG + D_GROUP
OFF_SMALL = OFF_GG + D_GROUP
D_IN_PACKED = OFF_SMALL + LANES
SM_DT = 0
SM_I = SM_DT + SSD_HEADS
SM_F = SM_I + MLSTM_HEADS
SM_AG = SM_F + MLSTM_HEADS
SM_END = SM_AG + GLA_GATE_RANK

MIX_SSD = 0
MIX_ML = D_GROUP
MIX_RG = 2 * D_GROUP
MIX_GLA = 3 * D_GROUP


def _dot(a, b):
    return jnp.dot(a, b, preferred_element_type=F32)


def _dot_nt(a, b):
    return lax.dot_general(a, b, (((1,), (1,)), ((), ())), preferred_element_type=F32)


def _dot_tn(a, b):
    return lax.dot_general(a, b, (((0,), (0,)), ((), ())), preferred_element_type=F32)


def _split3(v):
    hi = v.astype(BF16)
    r = v - hi.astype(F32)
    mid = r.astype(BF16)
    lo = (r - mid.astype(F32)).astype(BF16)
    return hi, mid, lo


def _sel_left(m01, v):
    hi, mid, lo = _split3(v)
    return _dot(m01, hi) + _dot(m01, mid) + _dot(m01, lo)


def _sel_right(v, m01):
    hi, mid, lo = _split3(v)
    return _dot(hi, m01) + _dot(mid, m01) + _dot(lo, m01)


def _layer_norm(x, g, b):
    mu = jnp.mean(x, -1, keepdims=True)
    xc = x - mu
    var = jnp.mean(xc * xc, -1, keepdims=True)
    return xc * lax.rsqrt(var + EPS) * g + b


def _chunk_last(v, cs):
    r, c = v.shape
    v3 = v.reshape(r // cs, cs, c)
    last = v3[:, cs - 1:cs, :]
    return jnp.broadcast_to(last, (r // cs, cs, c)).reshape(r, c)


def _mixer_kernel(*refs, nb, t, cs, nt, pre_ln, zero_init):
    r = nb * t
    ncl = t // cs
    it = iter(refs)
    x_ref = next(it)
    if pre_ln:
        lng_ref, lnb_ref = next(it), next(it)
    if not zero_init:
        (i_hT, i_sconv, i_C, i_n, i_m, i_rgh, i_rgconv, i_ST) = [next(it) for _ in range(8)]
    (w_in_ref, sbadd_ref, alog_ref, sconvw_ref, sconvb_ref, sD_ref, snw_ref, mnw_ref,
     rconvw_ref, rconvb_ref, rwa_ref, rba_ref, rwx_ref, rbx_ref, rlam_ref,
     gw2_ref, ggb_ref, gnw_ref, w_out_ref, ln1g_ref, ln1b_ref) = [next(it) for _ in range(21)]
    (o_x, o_hT, o_sconv, o_C, o_n, o_m, o_rgh, o_rgconv, o_ST) = [next(it) for _ in range(9)]
    (xn_ref, sp_ref, cum_ref, z_ref, xsf_ref, xsb_ref, xw_ref, bm_ref, cm_ref, ea_ref,
     q_ref, ks_ref, v_ref, o_ref, qt_ref, kt_ref, kd_ref, dch_ref, vg_ref, gg_ref,
     ra_ref, rb_ref, mix_ref, ubs_ref, ubr_ref,
     s_hT, s_sconv, s_C, s_n, s_m, s_rgh, s_rgconv, s_ST) = [next(it) for _ in range(33)]

    ti = pl.program_id(1)

    @pl.when(ti == 0)
    def _():
        if zero_init:
            s_hT[...] = jnp.zeros_like(s_hT)
            s_sconv[...] = jnp.zeros_like(s_sconv)
            s_C[...] = jnp.zeros_like(s_C)
            s_n[...] = jnp.zeros_like(s_n)
            s_m[...] = jnp.zeros_like(s_m)
            s_rgh[...] = jnp.zeros_like(s_rgh)
            s_rgconv[...] = jnp.zeros_like(s_rgconv)
            s_ST[...] = jnp.zeros_like(s_ST)
        else:
            s_hT[...] = i_hT[...]
            s_sconv[...] = jnp.zeros_like(s_sconv)
            s_sconv[:, SUBLANES - 3:SUBLANES, :] = i_sconv[...]
            s_C[...] = i_C[...]
            s_n[...] = jnp.zeros_like(s_n)
            s_n[:, 0:MLSTM_HEADS, :] = i_n[...]
            s_m[...] = jnp.zeros_like(s_m)
            s_m[:, 0:MLSTM_HEADS, :] = i_m[...]
            s_rgh[...] = i_rgh[...]
            s_rgconv[...] = jnp.zeros_like(s_rgconv)
            s_rgconv[:, SUBLANES - 3:SUBLANES, :] = i_rgconv[...]
            s_ST[...] = i_ST[...]

    x = x_ref[...].reshape(r, D_MODEL)
    if pre_ln:
        x = _layer_norm(x, lng_ref[...], lnb_ref[...])
    xn_ref[...] = x
    xb = x.astype(BF16)

    def seg(off, width):
        return _dot(xb, w_in_ref[:, off:off + width])

    lane = lax.broadcasted_iota(jnp.int32, (1, LANES), 1)

    s_raw = seg(OFF_SMALL, LANES)
    sb = s_raw + sbadd_ref[...]
    dt = jax.nn.softplus(sb)
    logf = jax.nn.log_sigmoid(sb)
    sp = jnp.where(lane < SM_I, dt, jnp.where(lane < SM_F, sb, jnp.where(lane < SM_AG, logf, 0.0)))
    sp_ref[...] = sp
    rr = lax.broadcasted_iota(jnp.int32, (r, r), 0)
    cc = lax.broadcasted_iota(jnp.int32, (r, r), 1)
    bdtri = jnp.where((rr // cs == cc // cs) & (cc <= rr), 1.0, 0.0).astype(BF16)
    cum = _sel_left(bdtri, sp)
    a_neg = -jnp.exp(alog_ref[...])
    cuma = jnp.where(lane < SM_I, cum * a_neg, cum)
    cum_ref[...] = cuma
    acum_last = _chunk_last(cuma, cs)
    head_lane = lane < SM_I
    ea8 = jnp.where(head_lane, jnp.exp(cuma), 0.0)
    wl8 = jnp.where(head_lane, jnp.exp(acum_last - cuma) * sp, 0.0)
    er = lax.broadcasted_iota(jnp.int32, (LANES, D_GROUP), 0)
    ec = lax.broadcasted_iota(jnp.int32, (LANES, D_GROUP), 1)
    expand = jnp.where(ec // SSD_HEAD_DIM == er, 1.0, 0.0).astype(BF16)
    ea_ref[...] = _sel_right(ea8, expand)
    wl_exp = _sel_right(wl8, expand)

    z_ref[...] = seg(OFF_Z, D_GROUP)
    xbc_raw = seg(OFF_XBC, SSD_CONV_DIM)
    cw = sconvw_ref[...]
    cb = sconvb_ref[...]
    for b in range(nb):
        rows = slice(b * t, (b + 1) * t)
        ubs_ref[0:SUBLANES, :] = s_sconv[b]
        ubs_ref[SUBLANES:SUBLANES + t, :] = xbc_raw[rows]
        acc = cb + ubs_ref[5:5 + t, :] * cw[0:1]
        for j in range(1, CONV_WIDTH):
            acc = acc + ubs_ref[5 + j:5 + j + t, :] * cw[j:j + 1]
        s_sconv[b, SUBLANES - 3:SUBLANES, :] = ubs_ref[5 + t:8 + t, :]
        xbc = jax.nn.silu(acc)
        xs = xbc[:, 0:D_GROUP]
        xsf_ref[rows, :] = xs
        xsb_ref[rows, :] = xs.astype(BF16)
        xw_ref[rows, :] = (xs * wl_exp[rows]).astype(BF16)
        bm_ref[rows, :] = xbc[:, D_GROUP:D_GROUP + LANES].astype(BF16)
        cm_ref[rows, :] = xbc[:, D_GROUP + LANES:D_GROUP + 2 * LANES].astype(BF16)

    q_ref[...] = seg(OFF_QM, D_GROUP).astype(BF16)
    ks_ref[...] = seg(OFF_KM, D_GROUP) * (MLSTM_DK ** -0.5)
    v_ref[...] = seg(OFF_VM, D_GROUP).astype(BF16)
    o_ref[...] = seg(OFF_OM, D_GROUP)

    la = jax.nn.log_sigmoid(_dot(s_raw.astype(BF16), gw2_ref[...]) + ggb_ref[...]) * (1.0 / GLA_TAU)
    bcum = _sel_left(bdtri, la)
    bl = _chunk_last(bcum, cs)
    qg = seg(OFF_QG, GLA_HEADS * GLA_DK) * (GLA_DK ** -0.5)
    kg = seg(OFF_KG, GLA_HEADS * GLA_DK)
    qt_ref[...] = (qg * jnp.exp(bcum)).astype(BF16)
    kt_ref[...] = (kg * jnp.exp(-bcum)).astype(BF16)
    kd_ref[...] = (kg * jnp.exp(bl - bcum)).astype(BF16)
    dch_ref[...] = jnp.exp(bl)
    vg_ref[...] = seg(OFF_VG, D_GROUP).astype(BF16)
    gg_ref[...] = seg(OFF_GG, D_GROUP)

    xr_raw = seg(OFF_XR, D_GROUP)
    yr = seg(OFF_YR, D_GROUP)
    rcw = rconvw_ref[...]
    rcb = rconvb_ref[...]
    log_sig_lam = jax.nn.log_sigmoid(rlam_ref[...])
    half = D_GROUP // 2
    for b in range(nb):
        rows = slice(b * t, (b + 1) * t)
        ubr_ref[0:SUBLANES, :] = s_rgconv[b]
        ubr_ref[SUBLANES:SUBLANES + t, :] = xr_raw[rows]
        xr = rcb + ubr_ref[5:5 + t, :] * rcw[0:1]
        for j in range(1, CONV_WIDTH):
            xr = xr + ubr_ref[5 + j:5 + j + t, :] * rcw[j:j + 1]
        s_rgconv[b, SUBLANES - 3:SUBLANES, :] = ubr_ref[5 + t:8 + t, :]
        xrb = xr.astype(BF16)
        ga = jnp.concatenate([_dot(xrb[:, 0:half], rwa_ref[0]), _dot(xrb[:, half:], rwa_ref[1])], axis=1)
        gx = jnp.concatenate([_dot(xrb[:, 0:half], rwx_ref[0]), _dot(xrb[:, half:], rwx_ref[1])], axis=1)
        rgate = jax.nn.sigmoid(ga + rba_ref[...])
        igate = jax.nn.sigmoid(gx + rbx_ref[...])
        log_a = RGLRU_C * rgate * log_sig_lam
        a_rg = jnp.exp(log_a)
        ra_ref[rows, :] = a_rg
        rb_ref[rows, :] = jnp.sqrt(-jnp.tanh(log_a) * (a_rg * a_rg + 1.0)) * (igate * xr)

    srow = lax.broadcasted_iota(jnp.int32, (SUBLANES, D_GROUP), 0)
    for b in range(nb):
        def rg_block(j, h, b=b):
            rs = pl.ds(pl.multiple_of(b * t + j * SUBLANES, SUBLANES), SUBLANES)
            a = ra_ref[rs, :]
            bb = rb_ref[rs, :]
            for d in (1, 2, 4):
                a_sh = pltpu.roll(a, d, 0)
                b_sh = pltpu.roll(bb, d, 0)
                ok = srow >= d
                bb = bb + a * jnp.where(ok, b_sh, 0.0)
                a = a * jnp.where(ok, a_sh, 1.0)
            hb = bb + a * h
            rb_ref[rs, :] = hb
            return hb[SUBLANES - 1:SUBLANES, :]
        s_rgh[b] = lax.fori_loop(0, t // SUBLANES, rg_block, s_rgh[b])
    mix_ref[:, MIX_RG:MIX_RG + D_GROUP] = (rb_ref[...] * jax.nn.gelu(yr)).astype(BF16)

    trow = lax.broadcasted_iota(jnp.int32, (cs, cs), 0)
    tcol = lax.broadcasted_iota(jnp.int32, (cs, cs), 1)
    tri = tcol <= trow
    lane_lo = lane < (LANES // 2)
    hrow = lax.broadcasted_iota(jnp.int32, (LANES, D_GROUP), 0)
    hcol = lax.broadcasted_iota(jnp.int32, (LANES, D_GROUP), 1)
    ssd_blockmask = (hrow // SSD_STATE) == (hcol // (D_GROUP // SSD_GROUPS))
    lane512_lo = (lax.broadcasted_iota(jnp.int32, (1, D_GROUP), 1) % LANES) < (LANES // 2)
    neg_inf = -jnp.inf
    inv_dv = 1.0 / MLSTM_DV

    def chunk_body(i, carry):
        r0 = pl.multiple_of(i * cs, cs)
        rs = pl.ds(r0, cs)
        sq = i // ncl
        sp_c = sp_ref[rs, :]
        cum_c = cum_ref[rs, :]
        sp_t = sp_c.T
        cum_t = cum_c.T

        cm_c = cm_ref[rs, :]
        bm_c = bm_ref[rs, :]
        zero_b = jnp.zeros_like(cm_c)
        g_mat = [_dot_nt(jnp.where(lane_lo, cm_c, zero_b), bm_c),
                 _dot_nt(jnp.where(lane_lo, zero_b, cm_c), bm_c)]
        h_t = s_hT[sq]
        y = _dot(cm_c, h_t.astype(BF16)) * ea_ref[rs, :]
        xs_b = xsb_ref[rs, :]
        y_pairs = []
        for hp in range(SSD_HEADS // 2):
            blk = xs_b[:, hp * LANES:(hp + 1) * LANES]
            halves = []
            for h2 in range(2):
                h = 2 * hp + h2
                col = cum_c[:, SM_DT + h:SM_DT + h + 1]
                row = cum_t[SM_DT + h:SM_DT + h + 1, :]
                dtrow = sp_t[SM_DT + h:SM_DT + h + 1, :]
                decay = jnp.exp(jnp.where(tri, col - row, neg_inf))
                m = (g_mat[h // (SSD_HEADS // SSD_GROUPS)] * decay * dtrow).astype(BF16)
                halves.append(_dot(m, blk))
            y_pairs.append(jnp.where(lane_lo, halves[0], halves[1]))
        y = y + jnp.concatenate(y_pairs, axis=1) + sD_ref[...] * xsf_ref[rs, :]
        yz = y * jax.nn.silu(z_ref[rs, :])
        yz = yz * lax.rsqrt(jnp.mean(yz * yz, -1, keepdims=True) + EPS) * snw_ref[...]
        mix_ref[rs, MIX_SSD:MIX_SSD + D_GROUP] = yz.astype(BF16)
        upd = _dot_tn(bm_c, xw_ref[rs, :])
        last_rows = pl.ds(pl.multiple_of(r0 + cs - SUBLANES, SUBLANES), SUBLANES)
        dch = ea_ref[last_rows, :][SUBLANES - 1:SUBLANES, :]
        s_hT[sq] = dch * h_t + jnp.where(ssd_blockmask, upd, 0.0)

        for h in range(MLSTM_HEADS):
            hl = slice(h * MLSTM_DK, (h + 1) * MLSTM_DK)
            q_h = q_ref[rs, hl]
            k_h = ks_ref[rs, hl]
            v_h = v_ref[rs, hl]
            bcol = cum_c[:, SM_F + h:SM_F + h + 1]
            brow = cum_t[SM_F + h:SM_F + h + 1, :]
            icol = sp_c[:, SM_I + h:SM_I + h + 1]
            irow = sp_t[SM_I + h:SM_I + h + 1, :]
            dmat = jnp.where(tri, bcol - brow + irow, neg_inf)
            m_intra = jnp.max(dmat, axis=1, keepdims=True)
            qk = _dot_nt(q_h, k_h.astype(BF16)) * jnp.exp(dmat - m_intra)
            num_intra = _dot(qk.astype(BF16), v_h)
            den_intra = jnp.sum(qk, axis=1, keepdims=True)
            b_last = bcol[cs - 1:cs, :]
            lw = b_last - bcol + icol
            m_loc = jnp.max(lw, axis=0, keepdims=True)
            wl = jnp.exp(lw - m_loc)
            kw = k_h * wl
            c_chunk = _dot_tn(kw.astype(BF16), v_h)
            n_chunk = jnp.sum(kw, axis=0, keepdims=True)
            c_s = s_C[sq, h]
            n_s = s_n[sq, h:h + 1, :]
            m_s = s_m[sq, h:h + 1, :][:, 0:1]
            m_new = jnp.maximum(b_last + m_s, m_loc)
            a_old = jnp.exp(b_last + m_s - m_new)
            a_new = jnp.exp(m_loc - m_new)
            s_C[sq, h] = a_old * c_s + a_new * c_chunk
            s_n[sq, h:h + 1, :] = a_old * n_s + a_new * n_chunk
            s_m[sq, h:h + 1, :] = jnp.broadcast_to(m_new, (1, LANES))
            g = bcol + m_s
            m_t = jnp.maximum(g, m_intra)
            a_inter = jnp.exp(g - m_t)
            a_intra = jnp.exp(m_intra - m_t)
            num = a_inter * _dot(q_h, c_s.astype(BF16)) + a_intra * num_intra
            qn = jnp.sum(q_h.astype(F32) * n_s, axis=1, keepdims=True)
            den = a_inter * qn + a_intra * den_intra
            hout = num / jnp.maximum(jnp.abs(den), jnp.exp(-m_t))
            hn = hout * lax.rsqrt(jnp.sum(hout * hout, -1, keepdims=True) * inv_dv + EPS)
            ym = jax.nn.sigmoid(o_ref[rs, hl]) * (hn * mnw_ref[:, hl])
            mix_ref[rs, MIX_ML + h * MLSTM_DV:MIX_ML + (h + 1) * MLSTM_DV] = ym.astype(BF16)

        for j in range(GLA_HEADS // 2):
            bl_ = slice(j * LANES, (j + 1) * LANES)
            qt = qt_ref[rs, bl_]
            kt = kt_ref[rs, bl_]
            kd = kd_ref[rs, bl_]
            st = s_ST[sq, j]
            st_b = st.astype(BF16)
            zq = jnp.zeros_like(qt)
            prods = []
            for h2 in range(2):
                h = 2 * j + h2
                hv = slice(h * GLA_DV, (h + 1) * GLA_DV)
                qm = jnp.where(lane_lo, qt, zq) if h2 == 0 else jnp.where(lane_lo, zq, qt)
                att = jnp.where(tri, _dot_nt(qm, kt), 0.0)
                v_h = vg_ref[rs, hv]
                o = _dot(att.astype(BF16), v_h) + _dot_nt(qm, st_b)
                on = o * lax.rsqrt(jnp.sum(o * o, -1, keepdims=True) * (1.0 / GLA_DV) + EPS)
                yg = on * gnw_ref[:, hv] * jax.nn.silu(gg_ref[rs, hv])
                mix_ref[rs, MIX_GLA + h * GLA_DV:MIX_GLA + (h + 1) * GLA_DV] = yg.astype(BF16)
                prods.append(_dot_tn(v_h, kd))
            dg = dch_ref[last_rows, bl_][SUBLANES - 1:SUBLANES, :]
            s_ST[sq, j] = dg * st + jnp.where(lane_lo, prods[0], prods[1])
        return carry

    lax.fori_loop(0, r // cs, chunk_body, 0)

    y1 = DEEPNORM_ALPHA * xn_ref[...] + _dot(mix_ref[...], w_out_ref[...])
    o_x[...] = _layer_norm(y1, ln1g_ref[...], ln1b_ref[...]).reshape(nb, t, D_MODEL)

    @pl.when(ti == nt - 1)
    def _():
        o_hT[...] = s_hT[...]
        o_sconv[...] = s_sconv[:, SUBLANES - 3:SUBLANES, :]
        o_C[...] = s_C[...]
        o_n[...] = s_n[:, 0:MLSTM_HEADS, :]
        o_m[...] = s_m[:, 0:MLSTM_HEADS, :]
        o_rgh[...] = s_rgh[...]
        o_rgconv[...] = s_rgconv[:, SUBLANES - 3:SUBLANES, :]
        o_ST[...] = s_ST[...]


def _const_spec(shape):
    nd = len(shape)
    return pl.BlockSpec(shape, lambda b, s, _nd=nd: (0,) * _nd, pipeline_mode=pl.Buffered(1))


def _mixer_call(x, ln_in, states, w, *, nb, t, cs):
    bsz, seq, _ = x.shape
    assert bsz % nb == 0 and seq % t == 0 and t % cs == 0 and t % SUBLANES == 0
    nt = seq // t
    r = nb * t
    pre_ln = ln_in is not None
    zero_init = states is None

    state_shapes = (
        (bsz, LANES, D_GROUP),
        (bsz, CONV_WIDTH - 1, SSD_CONV_DIM),
        (bsz, MLSTM_HEADS, MLSTM_DK, MLSTM_DV),
        (bsz, MLSTM_HEADS, MLSTM_DK),
        (bsz, MLSTM_HEADS, LANES),
        (bsz, 1, D_GROUP),
        (bsz, CONV_WIDTH - 1, D_GROUP),
        (bsz, GLA_HEADS // 2, GLA_DV, LANES),
    )

    def state_spec(shape):
        blk = (nb,) + shape[1:]
        nd = len(shape)
        return pl.BlockSpec(blk, lambda b, s, _nd=nd: (b,) + (0,) * (_nd - 1))

    x_spec = pl.BlockSpec((nb, t, D_MODEL), lambda b, s: (b, s, 0))
    args = [x]
    in_specs = [x_spec]
    if pre_ln:
        args += list(ln_in)
        in_specs += [_const_spec(a.shape) for a in ln_in]
    if not zero_init:
        args += list(states)
        in_specs += [state_spec(s) for s in state_shapes]
    args += list(w)
    in_specs += [_const_spec(a.shape) for a in w]

    out_shape = [jax.ShapeDtypeStruct(x.shape, F32)] + [jax.ShapeDtypeStruct(s, F32) for s in state_shapes]
    out_specs = [x_spec] + [state_spec(s) for s in state_shapes]

    vm = pltpu.VMEM
    scratch = [
        vm((r, D_MODEL), F32),
        vm((r, LANES), F32),
        vm((r, LANES), F32),
        vm((r, D_GROUP), F32),
        vm((r, D_GROUP), F32),
        vm((r, D_GROUP), BF16),
        vm((r, D_GROUP), BF16),
        vm((r, LANES), BF16),
        vm((r, LANES), BF16),
        vm((r, D_GROUP), F32),
        vm((r, D_GROUP), BF16),
        vm((r, D_GROUP), F32),
        vm((r, D_GROUP), BF16),
        vm((r, D_GROUP), F32),
        vm((r, GLA_HEADS * GLA_DK), BF16),
        vm((r, GLA_HEADS * GLA_DK), BF16),
        vm((r, GLA_HEADS * GLA_DK), BF16),
        vm((r, GLA_HEADS * GLA_DK), F32),
        vm((r, D_GROUP), BF16),
        vm((r, D_GROUP), F32),
        vm((r, D_GROUP), F32),
        vm((r, D_GROUP), F32),
        vm((r, D_MIX), BF16),
        vm((SUBLANES + t, SSD_CONV_DIM), F32),
        vm((SUBLANES + t, D_GROUP), F32),
        vm((nb, LANES, D_GROUP), F32),
        vm((nb, SUBLANES, SSD_CONV_DIM), F32),
        vm((nb, MLSTM_HEADS, MLSTM_DK, MLSTM_DV), F32),
        vm((nb, SUBLANES, MLSTM_DK), F32),
        vm((nb, SUBLANES, LANES), F32),
        vm((nb, 1, D_GROUP), F32),
        vm((nb, SUBLANES, D_GROUP), F32),
        vm((nb, GLA_HEADS // 2, GLA_DV, LANES), F32),
    ]

    kern = functools.partial(_mixer_kernel, nb=nb, t=t, cs=cs, nt=nt, pre_ln=pre_ln, zero_init=zero_init)
    return pl.pallas_call(
        kern,
        grid=(bsz // nb, nt),
        in_specs=in_specs,
        out_specs=out_specs,
        out_shape=out_shape,
        scratch_shapes=scratch,
        compiler_params=pltpu.CompilerParams(
            dimension_semantics=("arbitrary", "arbitrary"),
            vmem_limit_bytes=VMEM_LIMIT_BYTES),
        name="mixer_layer",
    )(*args)


MLP_FF_CHUNK = 1024


def _mlp_kernel(x_ref, w1_ref, b1_ref, w2_ref, b2_ref, g_ref, b_ref, o_ref):
    x = x_ref[...]
    xb = x.astype(BF16)
    acc = jnp.zeros(x.shape, F32)
    for c in range(D_FF // MLP_FF_CHUNK):
        cols = slice(c * MLP_FF_CHUNK, (c + 1) * MLP_FF_CHUNK)
        hid = _dot(xb, w1_ref[:, cols]) + b1_ref[:, cols]
        hid = jnp.square(jnp.maximum(hid, 0.0))
        acc = acc + _dot(hid.astype(BF16), w2_ref[cols, :])
    y = DEEPNORM_ALPHA * x + acc + b2_ref[...]
    o_ref[...] = _layer_norm(y, g_ref[...], b_ref[...])


def _mlp_call(x2d, w1, b1, w2, b2, g, b, *, tm):
    n = x2d.shape[0]
    assert n % tm == 0
    row_spec = pl.BlockSpec((tm, D_MODEL), lambda i: (i, 0))

    def cspec(shape):
        nd = len(shape)
        return pl.BlockSpec(shape, lambda i, _nd=nd: (0,) * _nd, pipeline_mode=pl.Buffered(1))

    return pl.pallas_call(
        _mlp_kernel,
        grid=(n // tm,),
        in_specs=[row_spec] + [cspec(a.shape) for a in (w1, b1, w2, b2, g, b)],
        out_specs=row_spec,
        out_shape=jax.ShapeDtypeStruct(x2d.shape, F32),
        compiler_params=pltpu.CompilerParams(
            dimension_semantics=("arbitrary",), vmem_limit_bytes=VMEM_LIMIT_BYTES),
        name="mlp_layer",
    )(x2d, w1, b1, w2, b2, g, b)


def _pack_w_in(w_in):
    def sl(k):
        return w_in[..., _IN_OFFS[k]:_IN_OFFS[k + 1]]
    (z, xbc, dt, qm, km, vm_, ifm, om, xr, yr, qg, kg, vg, ag, gg) = [sl(k) for k in range(15)]
    pad = jnp.zeros(w_in.shape[:-1] + (LANES - SM_END,), w_in.dtype)
    small = jnp.concatenate([dt, ifm, ag, pad], axis=-1)
    packed = jnp.concatenate([z, xbc, qm, km, vm_, om, xr, yr, qg, kg, vg, gg, small], axis=-1)
    return packed.astype(BF16)


def _lane_row(parts, width=LANES):
    v = jnp.concatenate(parts, axis=-1)
    v = jnp.pad(v, ((0, 0), (0, width - v.shape[-1])))
    return v[:, None, :]


def _block_diag_pairs(wg):
    l = wg.shape[0]
    nblk = RGLRU_BLOCKS // 2
    w5 = wg.reshape(l, 2, nblk, RGLRU_BLOCK_DIM, RGLRU_BLOCK_DIM)
    eye = jnp.eye(nblk, dtype=wg.dtype)
    bd = jnp.einsum('lhnkj,nm->lhnkmj', w5, eye)
    return bd.reshape(l, 2, nblk * RGLRU_BLOCK_DIM, nblk * RGLRU_BLOCK_DIM).astype(BF16)


def _ssd_state_to_kernel(h):
    bsz = h.shape[0]
    ht = jnp.transpose(h, (0, 3, 1, 2)).reshape(bsz, SSD_STATE, D_GROUP)
    hpg = D_GROUP // SSD_GROUPS
    col_group = (jnp.arange(D_GROUP) // hpg)[None, None, :]
    parts = [jnp.where(col_group == g, ht, 0.0) for g in range(SSD_GROUPS)]
    return jnp.concatenate(parts, axis=1)


def _ssd_state_from_kernel(ht):
    bsz = ht.shape[0]
    hpg = D_GROUP // SSD_GROUPS
    h5 = ht.reshape(bsz, SSD_GROUPS, SSD_STATE, SSD_GROUPS, hpg)
    h = jnp.stack([h5[:, g, :, g] for g in range(SSD_GROUPS)], axis=2)
    return jnp.transpose(h.reshape(bsz, SSD_STATE, SSD_HEADS, SSD_HEAD_DIM), (0, 2, 3, 1))


def _gla_state_to_kernel(s):
    bsz = s.shape[0]
    s5 = s.reshape(bsz, GLA_HEADS // 2, 2, GLA_DK, GLA_DV)
    return jnp.transpose(s5, (0, 1, 4, 2, 3)).reshape(bsz, GLA_HEADS // 2, GLA_DV, 2 * GLA_DK)


def _gla_state_from_kernel(st):
    bsz = st.shape[0]
    s5 = st.reshape(bsz, GLA_HEADS // 2, GLA_DV, 2, GLA_DK)
    return jnp.transpose(s5, (0, 1, 3, 4, 2)).reshape(bsz, GLA_HEADS, GLA_DK, GLA_DV)


PROMPT_TILE = 256
MLP_TILE = 512
SAMPLE_SEQS_PER_STEP = 8


def kernel(x_prompt, x_sample, state_ssd_h, state_ssd_conv, state_mlstm_C, state_mlstm_n, state_mlstm_m, state_rglru_h, state_rglru_conv, state_gla_S, ln_in_g, ln_in_b, w_in, ssd_conv_w, ssd_conv_b, ssd_dt_bias, ssd_A_log, ssd_D, ssd_norm_w, mlstm_if_b, mlstm_norm_w, rg_conv_w, rg_conv_b, rg_gate_a_w, rg_gate_a_b, rg_gate_x_w, rg_gate_x_b, rg_lambda, gla_gate_w2, gla_gate_b, gla_norm_w, w_out, ln1_g, ln1_b, mlp_w1, mlp_b1, mlp_w2, mlp_b2, ln2_g, ln2_b):
    nl = w_in.shape[0]
    bp, lp, _ = x_prompt.shape
    bs, ls, _ = x_sample.shape

    def row(a):
        return a[:, None, :]

    w_in_p = _pack_w_in(w_in)
    sb_add = _lane_row([ssd_dt_bias, mlstm_if_b])
    alog = _lane_row([ssd_A_log])
    sd_exp = row(jnp.repeat(ssd_D, SSD_HEAD_DIM, axis=-1))
    rwa = _block_diag_pairs(rg_gate_a_w)
    rwx = _block_diag_pairs(rg_gate_x_w)
    gw2 = jnp.pad(gla_gate_w2, ((0, 0), (SM_AG, LANES - SM_END), (0, 0))).astype(BF16)
    w_out_b = w_out.astype(BF16)
    w1_b = mlp_w1.astype(BF16)
    w2_b = mlp_w2.astype(BF16)
    ln_in = (ln_in_g[None, :], ln_in_b[None, :])

    def layer_weights(l):
        return (w_in_p[l], sb_add[l], alog[l], ssd_conv_w[l], row(ssd_conv_b)[l], sd_exp[l],
                row(ssd_norm_w)[l], row(mlstm_norm_w)[l], rg_conv_w[l], row(rg_conv_b)[l],
                rwa[l], row(rg_gate_a_b)[l], rwx[l], row(rg_gate_x_b)[l], row(rg_lambda)[l],
                gw2[l], row(gla_gate_b)[l], row(gla_norm_w)[l], w_out_b[l], row(ln1_g)[l], row(ln1_b)[l])

    def mlp(x3, l, tm):
        shp = x3.shape
        y = _mlp_call(x3.reshape(-1, D_MODEL), w1_b[l], row(mlp_b1)[l], w2_b[l], row(mlp_b2)[l],
                      row(ln2_g)[l], row(ln2_b)[l], tm=tm)
        return y.reshape(shp)

    def finish_states(st, bsz):
        hT, sconv, c, n, m, rgh, rgconv, gst = st
        return (_ssd_state_from_kernel(hT), sconv, c, n, m[:, :, 0], rgh.reshape(bsz, D_GROUP), rgconv,
                _gla_state_from_kernel(gst))

    xp, xs = x_prompt, x_sample
    p_states, s_states = [], []
    for l in range(nl):
        w = layer_weights(l)
        first = ln_in if l == 0 else None
        outs = _mixer_call(xp, first, None, w, nb=1, t=PROMPT_TILE, cs=CHUNK)
        xp = mlp(outs[0], l, MLP_TILE)
        p_states.append(finish_states(outs[1:], bp))

        cached = (
            _ssd_state_to_kernel(state_ssd_h[l]), state_ssd_conv[l], state_mlstm_C[l], state_mlstm_n[l],
            jnp.broadcast_to(state_mlstm_m[l][:, :, None], (bs, MLSTM_HEADS, LANES)),
            state_rglru_h[l][:, None, :], state_rglru_conv[l], _gla_state_to_kernel(state_gla_S[l]))
        cs_s = CHUNK if ls % CHUNK == 0 else ls
        outs = _mixer_call(xs, first, cached, w, nb=SAMPLE_SEQS_PER_STEP, t=ls, cs=cs_s)
        xs = mlp(outs[0], l, bs * ls)
        s_states.append(finish_states(outs[1:], bs))

    p_stacked = [jnp.stack(v, axis=0) for v in zip(*p_states)]
    s_stacked = [jnp.stack(v, axis=0) for v in zip(*s_states)]
    return (xp, xs, *p_stacked, *s_stacked)
```

```python
import functools

import jax
import jax.numpy as jnp
from jax import lax
from jax.experimental import pallas as pl
from jax.experimental.pallas import tpu as pltpu

F32 = jnp.float32
BF16 = jnp.bfloat16

D_MODEL = 1024
DEPTH = 4
CHUNK = 64
CONV_WIDTH = 4
EPS = 1e-5
D_MIX = 2 * D_MODEL
D_GROUP = D_MIX // 4
SSD_HEAD_DIM = 64
SSD_HEADS = D_GROUP // SSD_HEAD_DIM
SSD_GROUPS = 2
SSD_STATE = 64
SSD_CONV_DIM = D_GROUP + 2 * SSD_GROUPS * SSD_STATE
MLSTM_HEADS = 4
MLSTM_DK = D_GROUP // MLSTM_HEADS
MLSTM_DV = D_GROUP // MLSTM_HEADS
RGLRU_BLOCKS = 8
RGLRU_BLOCK_DIM = D_GROUP // RGLRU_BLOCKS
RGLRU_C = 8.0
GLA_HEADS = 4
GLA_DV = D_GROUP // GLA_HEADS
GLA_DK = GLA_DV // 2
GLA_GATE_RANK = 16
GLA_TAU = 16.0
D_FF = 4 * D_MODEL
DEEPNORM_ALPHA = (2 * DEPTH) ** 0.25

IN_SPLIT_SIZES = (
    D_GROUP, SSD_CONV_DIM, SSD_HEADS,
    MLSTM_HEADS * MLSTM_DK, MLSTM_HEADS * MLSTM_DK, MLSTM_HEADS * MLSTM_DV, 2 * MLSTM_HEADS,
    MLSTM_HEADS * MLSTM_DV, D_GROUP, D_GROUP,
    GLA_HEADS * GLA_DK, GLA_HEADS * GLA_DK, GLA_HEADS * GLA_DV, GLA_GATE_RANK, GLA_HEADS * GLA_DV,
)
_IN_OFFS = [0]
for _s in IN_SPLIT_SIZES:
    _IN_OFFS.append(_IN_OFFS[-1] + _s)

LANES = 128
SUBLANES = 8
VMEM_LIMIT_BYTES = 56 * 1024 * 1024

OFF_Z = 0
OFF_XBC = OFF_Z + D_GROUP
OFF_QM = OFF_XBC + SSD_CONV_DIM
OFF_KM = OFF_QM + D_GROUP
OFF_VM = OFF_KM + D_GROUP
OFF_OM = OFF_VM + D_GROUP
OFF_XR = OFF_OM + D_GROUP
OFF_YR = OFF_XR + D_GROUP
OFF_QG = OFF_YR + D_GROUP
OFF_KG = OFF_QG + GLA_HEADS * GLA_DK
OFF_VG = OFF_KG + GLA_HEADS * GLA_DK
OFF_GG = OFF_VG + D_GROUP
OFF_SMALL = OFF_GG + D_GROUP
D_IN_PACKED = OFF_SMALL + LANES
SM_DT = 0
SM_I = SM_DT + SSD_HEADS
SM_F = SM_I + MLSTM_HEADS
SM_AG = SM_F + MLSTM_HEADS
SM_END = SM_AG + GLA_GATE_RANK

MIX_SSD = 0
MIX_ML = D_GROUP
MIX_RG = 2 * D_GROUP
MIX_GLA = 3 * D_GROUP

OUT_PROJ_BLOCKS = 4
RG_BLOCKS_PER_STAGE = 4


def _dot(a, b):
    return jnp.dot(a, b, preferred_element_type=F32)


def _dot_nt(a, b):
    return lax.dot_general(a, b, (((1,), (1,)), ((), ())), preferred_element_type=F32)


def _dot_tn(a, b):
    return lax.dot_general(a, b, (((0,), (0,)), ((), ())), preferred_element_type=F32)


def _split3(v):
    hi = v.astype(BF16)
    r = v - hi.astype(F32)
    mid = r.astype(BF16)
    lo = (r - mid.astype(F32)).astype(BF16)
    return hi, mid, lo


def _sel_left(m01, v):
    hi, mid, lo = _split3(v)
    return _dot(m01, hi) + _dot(m01, mid) + _dot(m01, lo)


def _sel_right(v, m01):
    hi, mid, lo = _split3(v)
    return _dot(hi, m01) + _dot(mid, m01) + _dot(lo, m01)


def _layer_norm(x, g, b):
    mu = jnp.mean(x, -1, keepdims=True)
    xc = x - mu
    var = jnp.mean(xc * xc, -1, keepdims=True)
    return xc * lax.rsqrt(var + EPS) * g + b


def _chunk_last(v, cs):
    r, c = v.shape
    v3 = v.reshape(r // cs, cs, c)
    last = v3[:, cs - 1:cs, :]
    return jnp.broadcast_to(last, (r // cs, cs, c)).reshape(r, c)


def _run_interleaved(weighted_gens):
    live = [[g, w] for g, w in weighted_gens]
    while live:
        for entry in list(live):
            g, w = entry
            for _ in range(w):
                try:
                    next(g)
                except StopIteration:
                    live.remove(entry)
                    break


def _mixer_kernel(*refs, nb, t, cs, nt, nhalf, pre_ln, zero_init):
    r = nb * t
    rh = r // nhalf
    it = iter(refs)
    x_ref = next(it)
    if pre_ln:
        lng_ref, lnb_ref = next(it), next(it)
    if not zero_init:
        (i_hT, i_sconv, i_C, i_n, i_m, i_rgh, i_rgconv, i_ST) = [next(it) for _ in range(8)]
    (w_in_ref, sbadd_ref, alog_ref, sconvw_ref, sconvb_ref, sD_ref, snw_ref, mnw_ref,
     rconvw_ref, rconvb_ref, rwa_ref, rba_ref, rwx_ref, rbx_ref, rlam_ref,
     gw2_ref, ggb_ref, gnw_ref, w_out_ref, ln1g_ref, ln1b_ref) = [next(it) for _ in range(21)]
    (o_x, o_hT, o_sconv, o_C, o_n, o_m, o_rgh, o_rgconv, o_ST) = [next(it) for _ in range(9)]
    (xn_ref, xb_ref, sp_ref, cum_ref, z_ref, xsf_ref, xsb_ref, xw_ref, bm_ref, cm_ref, ea_ref,
     q_ref, ks_ref, v_ref, o_ref, qt_ref, kt_ref, kd_ref, dch_ref, vg_ref, gg_ref,
     ra_ref, rb_ref, mix_ref, ubs_ref, ubr_ref,
     s_hT, s_sconv, s_C, s_n, s_m, s_rgh, s_rgconv, s_ST) = [next(it) for _ in range(34)]

    ti = pl.program_id(1)

    @pl.when(ti == 0)
    def _():
        if zero_init:
            s_hT[...] = jnp.zeros_like(s_hT)
            s_sconv[...] = jnp.zeros_like(s_sconv)
            s_C[...] = jnp.zeros_like(s_C)
            s_n[...] = jnp.zeros_like(s_n)
            s_m[...] = jnp.zeros_like(s_m)
            s_rgh[...] = jnp.zeros_like(s_rgh)
            s_rgconv[...] = jnp.zeros_like(s_rgconv)
            s_ST[...] = jnp.zeros_like(s_ST)
        else:
            s_hT[...] = i_hT[...]
            s_sconv[...] = jnp.zeros_like(s_sconv)
            s_sconv[:, SUBLANES - 3:SUBLANES, :] = i_sconv[...]
            s_C[...] = i_C[...]
            s_n[...] = jnp.zeros_like(s_n)
            s_n[:, 0:MLSTM_HEADS, :] = i_n[...]
            s_m[...] = jnp.zeros_like(s_m)
            s_m[:, 0:MLSTM_HEADS, :] = i_m[...]
            s_rgh[...] = i_rgh[...]
            s_rgconv[...] = jnp.zeros_like(s_rgconv)
            s_rgconv[:, SUBLANES - 3:SUBLANES, :] = i_rgconv[...]
            s_ST[...] = i_ST[...]

    lane = lax.broadcasted_iota(jnp.int32, (1, LANES), 1)
    lane_lo = lane < (LANES // 2)
    head_lane = lane < SM_I
    rr = lax.broadcasted_iota(jnp.int32, (rh, rh), 0)
    cc = lax.broadcasted_iota(jnp.int32, (rh, rh), 1)
    bdtri = jnp.where((rr // cs == cc // cs) & (cc <= rr), 1.0, 0.0).astype(BF16)
    er = lax.broadcasted_iota(jnp.int32, (LANES, D_GROUP), 0)
    ec = lax.broadcasted_iota(jnp.int32, (LANES, D_GROUP), 1)
    expand = jnp.where(ec // SSD_HEAD_DIM == er, 1.0, 0.0).astype(BF16)
    ssd_blockmask = (er // SSD_STATE) == (ec // (D_GROUP // SSD_GROUPS))
    trow = lax.broadcasted_iota(jnp.int32, (cs, cs), 0)
    tcol = lax.broadcasted_iota(jnp.int32, (cs, cs), 1)
    tri = tcol <= trow
    srow = lax.broadcasted_iota(jnp.int32, (SUBLANES, D_GROUP), 0)
    neg_inf = -jnp.inf
    inv_dv = 1.0 / MLSTM_DV
    half_w = D_GROUP // 2

    def segments(hf):
        if nhalf == 1:
            return [(b, b * t, t) for b in range(nb)]
        return [(0, hf * rh, rh)]

    def load_x(hf):
        if nb == 1:
            return x_ref[0, hf * rh:(hf + 1) * rh, :]
        return x_ref[...].reshape(r, D_MODEL)

    def store_out(hf, cols, val):
        if nb == 1:
            o_x[0, hf * rh:(hf + 1) * rh, cols] = val
        else:
            o_x[:, :, cols] = val.reshape(nb, t, val.shape[-1])

    def load_out(hf):
        if nb == 1:
            return o_x[0, hf * rh:(hf + 1) * rh, :]
        return o_x[...].reshape(r, D_MODEL)

    def prep(hf):
        h0 = hf * rh
        hr = slice(h0, h0 + rh)
        x = load_x(hf)
        if pre_ln:
            x = _layer_norm(x, lng_ref[...], lnb_ref[...])
        xn_ref[hr, :] = x
        xb_ref[hr, :] = x.astype(BF16)
        yield

        def seg(off, width):
            return _dot(xb_ref[hr, :], w_in_ref[:, off:off + width])

        z_ref[hr, :] = seg(OFF_Z, D_GROUP)
        yield
        xbc_raw = seg(OFF_XBC, SSD_CONV_DIM)
        cw = sconvw_ref[...]
        cb = sconvb_ref[...]
        for (b, s0, sl) in segments(hf):
            rows = slice(s0, s0 + sl)
            ubs_ref[0:SUBLANES, :] = s_sconv[b]
            ubs_ref[SUBLANES:SUBLANES + sl, :] = xbc_raw[s0 - h0:s0 - h0 + sl]
            acc = cb + ubs_ref[5:5 + sl, :] * cw[0:1]
            for j in range(1, CONV_WIDTH):
                acc = acc + ubs_ref[5 + j:5 + j + sl, :] * cw[j:j + 1]
            s_sconv[b, SUBLANES - 3:SUBLANES, :] = ubs_ref[5 + sl:8 + sl, :]
            xbc = jax.nn.silu(acc)
            xs = xbc[:, 0:D_GROUP]
            xsf_ref[rows, :] = xs
            xsb_ref[rows, :] = xs.astype(BF16)
            bm_ref[rows, :] = xbc[:, D_GROUP:D_GROUP + LANES].astype(BF16)
            cm_ref[rows, :] = xbc[:, D_GROUP + LANES:D_GROUP + 2 * LANES].astype(BF16)
        yield

        s_raw = seg(OFF_SMALL, LANES)
        sb = s_raw + sbadd_ref[...]
        dt = jax.nn.softplus(sb)
        logf = jax.nn.log_sigmoid(sb)
        sp = jnp.where(lane < SM_I, dt, jnp.where(lane < SM_F, sb, jnp.where(lane < SM_AG, logf, 0.0)))
        sp_ref[hr, :] = sp
        cum = _sel_left(bdtri, sp)
        a_neg = -jnp.exp(alog_ref[...])
        cuma = jnp.where(head_lane, cum * a_neg, cum)
        cum_ref[hr, :] = cuma
        acum_last = _chunk_last(cuma, cs)
        ea8 = jnp.where(head_lane, jnp.exp(cuma), 0.0)
        wl8 = jnp.where(head_lane, jnp.exp(acum_last - cuma) * sp, 0.0)
        yield
        ea_ref[hr, :] = _sel_right(ea8, expand)
        wl_exp = _sel_right(wl8, expand)
        xw_ref[hr, :] = (xsf_ref[hr, :] * wl_exp).astype(BF16)
        yield

        la = jax.nn.log_sigmoid(_dot(s_raw.astype(BF16), gw2_ref[...]) + ggb_ref[...]) * (1.0 / GLA_TAU)
        bcum = _sel_left(bdtri, la)
        bl = _chunk_last(bcum, cs)
        dch_ref[hr, :] = jnp.exp(bl)
        yield
        qg = seg(OFF_QG, GLA_HEADS * GLA_DK) * (GLA_DK ** -0.5)
        qt_ref[hr, :] = (qg * jnp.exp(bcum)).astype(BF16)
        kg = seg(OFF_KG, GLA_HEADS * GLA_DK)
        kt_ref[hr, :] = (kg * jnp.exp(-bcum)).astype(BF16)
        kd_ref[hr, :] = (kg * jnp.exp(bl - bcum)).astype(BF16)
        yield
        vg_ref[hr, :] = seg(OFF_VG, D_GROUP).astype(BF16)
        yield
        gg_ref[hr, :] = seg(OFF_GG, D_GROUP)
        yield

        q_ref[hr, :] = seg(OFF_QM, D_GROUP).astype(BF16)
        yield
        ks_ref[hr, :] = seg(OFF_KM, D_GROUP) * (MLSTM_DK ** -0.5)
        yield
        v_ref[hr, :] = seg(OFF_VM, D_GROUP).astype(BF16)
        yield
        o_ref[hr, :] = seg(OFF_OM, D_GROUP)
        yield

        xr_raw = seg(OFF_XR, D_GROUP)
        rcw = rconvw_ref[...]
        rcb = rconvb_ref[...]
        log_sig_lam = jax.nn.log_sigmoid(rlam_ref[...])
        for (b, s0, sl) in segments(hf):
            rows = slice(s0, s0 + sl)
            ubr_ref[0:SUBLANES, :] = s_rgconv[b]
            ubr_ref[SUBLANES:SUBLANES + sl, :] = xr_raw[s0 - h0:s0 - h0 + sl]
            xr = rcb + ubr_ref[5:5 + sl, :] * rcw[0:1]
            for j in range(1, CONV_WIDTH):
                xr = xr + ubr_ref[5 + j:5 + j + sl, :] * rcw[j:j + 1]
            s_rgconv[b, SUBLANES - 3:SUBLANES, :] = ubr_ref[5 + sl:8 + sl, :]
            xrb = xr.astype(BF16)
            ga = jnp.concatenate([_dot(xrb[:, 0:half_w], rwa_ref[0]), _dot(xrb[:, half_w:], rwa_ref[1])], axis=1)
            gx = jnp.concatenate([_dot(xrb[:, 0:half_w], rwx_ref[0]), _dot(xrb[:, half_w:], rwx_ref[1])], axis=1)
            rgate = jax.nn.sigmoid(ga + rba_ref[...])
            igate = jax.nn.sigmoid(gx + rbx_ref[...])
            log_a = RGLRU_C * rgate * log_sig_lam
            a_rg = jnp.exp(log_a)
            ra_ref[rows, :] = a_rg
            rb_ref[rows, :] = jnp.sqrt(-jnp.tanh(log_a) * (a_rg * a_rg + 1.0)) * (igate * xr)
        yield
        for (b, s0, sl) in segments(hf):
            h = s_rgh[b]
            for jb in range(sl // SUBLANES):
                rs = slice(s0 + jb * SUBLANES, s0 + (jb + 1) * SUBLANES)
                a = ra_ref[rs, :]
                bb = rb_ref[rs, :]
                for d in (1, 2, 4):
                    a_sh = pltpu.roll(a, d, 0)
                    b_sh = pltpu.roll(bb, d, 0)
                    ok = srow >= d
                    bb = bb + a * jnp.where(ok, b_sh, 0.0)
                    a = a * jnp.where(ok, a_sh, 1.0)
                hb = bb + a * h
                rb_ref[rs, :] = hb
                h = hb[SUBLANES - 1:SUBLANES, :]
                if jb % RG_BLOCKS_PER_STAGE == RG_BLOCKS_PER_STAGE - 1:
                    yield
            s_rgh[b] = h
        yr = seg(OFF_YR, D_GROUP)
        mix_ref[hr, MIX_RG:MIX_RG + D_GROUP] = (rb_ref[hr, :] * jax.nn.gelu(yr)).astype(BF16)
        yield

    def chunk_plan(hf):
        ids = list(range(hf * rh // cs, (hf + 1) * rh // cs))
        rows_of = {i: slice(i * cs, (i + 1) * cs) for i in ids}
        last_rows_of = {i: slice((i + 1) * cs - SUBLANES, (i + 1) * cs) for i in ids}
        seq_of = {i: (i * cs) // t for i in ids}
        load_at = {i: (i == ids[0]) or ((i * cs) % t == 0) for i in ids}
        store_at = {i: (i == ids[-1]) or (((i + 1) * cs) % t == 0) for i in ids}
        return ids, rows_of, last_rows_of, seq_of, load_at, store_at

    def mix_gens(hf):
        ids, rows_of, last_rows_of, seq_of, load_at, store_at = chunk_plan(hf)
        shared = {}

        def transposed():
            if not shared:
                shared['sp_c'] = {i: sp_ref[rows_of[i], :] for i in ids}
                shared['cum_c'] = {i: cum_ref[rows_of[i], :] for i in ids}
                shared['sp_t'] = {i: shared['sp_c'][i].T for i in ids}
                shared['cum_t'] = {i: shared['cum_c'][i].T for i in ids}
            return shared['sp_c'], shared['cum_c'], shared['sp_t'], shared['cum_t']

        def ssd():
            sp_c, cum_c, sp_t, cum_t = transposed()
            cm_c = {i: cm_ref[rows_of[i], :] for i in ids}
            bm_c = {i: bm_ref[rows_of[i], :] for i in ids}
            zero_b = jnp.zeros((cs, LANES), BF16)
            g_mat = {i: [_dot_nt(jnp.where(lane_lo, cm_c[i], zero_b), bm_c[i]),
                         _dot_nt(jnp.where(lane_lo, zero_b, cm_c[i]), bm_c[i])] for i in ids}
            upd = {i: _dot_tn(bm_c[i], xw_ref[rows_of[i], :]) for i in ids}
            yield
            m_ssd = {}
            for i in ids:
                per_head = []
                for h in range(SSD_HEADS):
                    col = cum_c[i][:, SM_DT + h:SM_DT + h + 1]
                    row = cum_t[i][SM_DT + h:SM_DT + h + 1, :]
                    dtrow = sp_t[i][SM_DT + h:SM_DT + h + 1, :]
                    decay = jnp.exp(jnp.where(tri, col - row, neg_inf))
                    per_head.append((g_mat[i][h // (SSD_HEADS // SSD_GROUPS)] * decay * dtrow).astype(BF16))
                m_ssd[i] = per_head
                if i % 2 == 1:
                    yield
            y_intra = {}
            for i in ids:
                xs_b = xsb_ref[rows_of[i], :]
                pairs = []
                for hp in range(SSD_HEADS // 2):
                    blk = xs_b[:, hp * LANES:(hp + 1) * LANES]
                    pairs.append(jnp.where(lane_lo, _dot(m_ssd[i][2 * hp], blk), _dot(m_ssd[i][2 * hp + 1], blk)))
                y_intra[i] = jnp.concatenate(pairs, axis=1)
            yield
            h_in = {}
            h_t = None
            for i in ids:
                if load_at[i]:
                    h_t = s_hT[seq_of[i]]
                h_in[i] = h_t
                dch = ea_ref[last_rows_of[i], :][SUBLANES - 1:SUBLANES, :]
                h_t = dch * h_t + jnp.where(ssd_blockmask, upd[i], 0.0)
                if store_at[i]:
                    s_hT[seq_of[i]] = h_t
            yield
            for i in ids:
                rs = rows_of[i]
                y = _dot(cm_c[i], h_in[i].astype(BF16)) * ea_ref[rs, :]
                y = y + y_intra[i] + sD_ref[...] * xsf_ref[rs, :]
                yz = y * jax.nn.silu(z_ref[rs, :])
                yz = yz * lax.rsqrt(jnp.mean(yz * yz, -1, keepdims=True) + EPS) * snw_ref[...]
                mix_ref[rs, MIX_SSD:MIX_SSD + D_GROUP] = yz.astype(BF16)
                if i % 2 == 1:
                    yield

        def mlstm(h):
            sp_c, cum_c, sp_t, cum_t = transposed()
            hl = slice(h * MLSTM_DK, (h + 1) * MLSTM_DK)
            q_h = {i: q_ref[rows_of[i], hl] for i in ids}
            v_h = {i: v_ref[rows_of[i], hl] for i in ids}
            k_h = {i: ks_ref[rows_of[i], hl] for i in ids}
            bcol = {i: cum_c[i][:, SM_F + h:SM_F + h + 1] for i in ids}
            dmat, m_intra, b_last, m_loc, kw = {}, {}, {}, {}, {}
            for i in ids:
                brow = cum_t[i][SM_F + h:SM_F + h + 1, :]
                icol = sp_c[i][:, SM_I + h:SM_I + h + 1]
                irow = sp_t[i][SM_I + h:SM_I + h + 1, :]
                dm = jnp.where(tri, bcol[i] - brow + irow, neg_inf)
                dmat[i] = dm
                m_intra[i] = jnp.max(dm, axis=1, keepdims=True)
                bl1 = bcol[i][cs - 1:cs, :]
                b_last[i] = bl1
                lw = bl1 - bcol[i] + icol
                ml = jnp.max(lw, axis=0, keepdims=True)
                m_loc[i] = ml
                kw[i] = k_h[i] * jnp.exp(lw - ml)
            qk_raw = {i: _dot_nt(q_h[i], k_h[i].astype(BF16)) for i in ids}
            c_chunk = {i: _dot_tn(kw[i].astype(BF16), v_h[i]) for i in ids}
            n_chunk = {i: jnp.sum(kw[i], axis=0, keepdims=True) for i in ids}
            yield
            qk = {i: qk_raw[i] * jnp.exp(dmat[i] - m_intra[i]) for i in ids}
            num_intra = {i: _dot(qk[i].astype(BF16), v_h[i]) for i in ids}
            den_intra = {i: jnp.sum(qk[i], axis=1, keepdims=True) for i in ids}
            yield
            c_in, n_in, m_in = {}, {}, {}
            c_s = n_s = m_s = None
            for i in ids:
                sq = seq_of[i]
                if load_at[i]:
                    c_s = s_C[sq, h]
                    n_s = s_n[sq, h:h + 1, :]
                    m_s = s_m[sq, h:h + 1, :][:, 0:1]
                c_in[i], n_in[i], m_in[i] = c_s, n_s, m_s
                m_new = jnp.maximum(b_last[i] + m_s, m_loc[i])
                a_old = jnp.exp(b_last[i] + m_s - m_new)
                a_new = jnp.exp(m_loc[i] - m_new)
                c_s = a_old * c_s + a_new * c_chunk[i]
                n_s = a_old * n_s + a_new * n_chunk[i]
                m_s = m_new
                if store_at[i]:
                    s_C[sq, h] = c_s
                    s_n[sq, h:h + 1, :] = n_s
                    s_m[sq, h:h + 1, :] = jnp.broadcast_to(m_s, (1, LANES))
            yield
            inter = {i: _dot(q_h[i], c_in[i].astype(BF16)) for i in ids}
            qn = {i: jnp.sum(q_h[i].astype(F32) * n_in[i], axis=1, keepdims=True) for i in ids}
            yield
            for i in ids:
                rs = rows_of[i]
                g = bcol[i] + m_in[i]
                m_t = jnp.maximum(g, m_intra[i])
                a_inter = jnp.exp(g - m_t)
                a_intra = jnp.exp(m_intra[i] - m_t)
                num = a_inter * inter[i] + a_intra * num_intra[i]
                den = a_inter * qn[i] + a_intra * den_intra[i]
                hout = num / jnp.maximum(jnp.abs(den), jnp.exp(-m_t))
                hn = hout * lax.rsqrt(jnp.sum(hout * hout, -1, keepdims=True) * inv_dv + EPS)
                ym = jax.nn.sigmoid(o_ref[rs, hl]) * (hn * mnw_ref[:, hl])
                mix_ref[rs, MIX_ML + h * MLSTM_DV:MIX_ML + (h + 1) * MLSTM_DV] = ym.astype(BF16)
            yield

        def gla(j):
            bl_ = slice(j * LANES, (j + 1) * LANES)
            zq = jnp.zeros((cs, LANES), BF16)
            qt = {i: qt_ref[rows_of[i], bl_] for i in ids}
            kt = {i: kt_ref[rows_of[i], bl_] for i in ids}
            kd = {i: kd_ref[rows_of[i], bl_] for i in ids}
            qm = {i: [jnp.where(lane_lo, qt[i], zq), jnp.where(lane_lo, zq, qt[i])] for i in ids}
            v_h = {i: [vg_ref[rows_of[i], (2 * j + h2) * GLA_DV:(2 * j + h2 + 1) * GLA_DV] for h2 in range(2)]
                   for i in ids}
            att_raw = {i: [_dot_nt(qm[i][h2], kt[i]) for h2 in range(2)] for i in ids}
            prods = {i: [_dot_tn(v_h[i][h2], kd[i]) for h2 in range(2)] for i in ids}
            yield
            att = {i: [jnp.where(tri, att_raw[i][h2], 0.0).astype(BF16) for h2 in range(2)] for i in ids}
            o_intra = {i: [_dot(att[i][h2], v_h[i][h2]) for h2 in range(2)] for i in ids}
            yield
            st_in = {}
            st = None
            for i in ids:
                sq = seq_of[i]
                if load_at[i]:
                    st = s_ST[sq, j]
                st_in[i] = st
                dg = dch_ref[last_rows_of[i], bl_][SUBLANES - 1:SUBLANES, :]
                st = dg * st + jnp.where(lane_lo, prods[i][0], prods[i][1])
                if store_at[i]:
                    s_ST[sq, j] = st
            yield
            o_inter = {i: [_dot_nt(qm[i][h2], st_in[i].astype(BF16)) for h2 in range(2)] for i in ids}
            yield
            for i in ids:
                rs = rows_of[i]
                for h2 in range(2):
                    h = 2 * j + h2
                    hv = slice(h * GLA_DV, (h + 1) * GLA_DV)
                    o = o_intra[i][h2] + o_inter[i][h2]
                    on = o * lax.rsqrt(jnp.sum(o * o, -1, keepdims=True) * (1.0 / GLA_DV) + EPS)
                    yg = on * gnw_ref[:, hv] * jax.nn.silu(gg_ref[rs, hv])
                    mix_ref[rs, MIX_GLA + h * GLA_DV:MIX_GLA + (h + 1) * GLA_DV] = yg.astype(BF16)
            yield

        return ([ssd()] + [mlstm(h) for h in range(MLSTM_HEADS)] + [gla(j) for j in range(GLA_HEADS // 2)])

    def outp(hf):
        hr = slice(hf * rh, (hf + 1) * rh)
        wblk = D_MODEL // OUT_PROJ_BLOCKS
        for c in range(OUT_PROJ_BLOCKS):
            cols = slice(c * wblk, (c + 1) * wblk)
            y1 = DEEPNORM_ALPHA * xn_ref[hr, cols] + _dot(mix_ref[hr, :], w_out_ref[:, cols])
            store_out(hf, cols, y1)
            yield
        store_out(hf, slice(0, D_MODEL), _layer_norm(load_out(hf), ln1g_ref[...], ln1b_ref[...]))
        yield

    if nhalf == 1:
        _run_interleaved([(prep(0), 1)])
        _run_interleaved([(g, 1) for g in mix_gens(0)])
        _run_interleaved([(outp(0), 1)])
    else:
        _run_interleaved([(prep(0), 1)])
        _run_interleaved([(prep(1), 4)] + [(g, 1) for g in mix_gens(0)])
        _run_interleaved([(outp(0), 1)] + [(g, 1) for g in mix_gens(1)])
        _run_interleaved([(outp(1), 1)])

    @pl.when(ti == nt - 1)
    def _():
        o_hT[...] = s_hT[...]
        o_sconv[...] = s_sconv[:, SUBLANES - 3:SUBLANES, :]
        o_C[...] = s_C[...]
        o_n[...] = s_n[:, 0:MLSTM_HEADS, :]
        o_m[...] = s_m[:, 0:MLSTM_HEADS, :]
        o_rgh[...] = s_rgh[...]
        o_rgconv[...] = s_rgconv[:, SUBLANES - 3:SUBLANES, :]
        o_ST[...] = s_ST[...]


def _const_spec(shape):
    nd = len(shape)
    return pl.BlockSpec(shape, lambda b, s, _nd=nd: (0,) * _nd, pipeline_mode=pl.Buffered(1))


def _mixer_call(x, ln_in, states, w, *, nb, t, cs, nhalf):
    bsz, seq, _ = x.shape
    assert bsz % nb == 0 and seq % t == 0 and t % SUBLANES == 0
    assert nhalf == 1 or nb == 1
    nt = seq // t
    r = nb * t
    assert r % nhalf == 0 and (r // nhalf) % cs == 0 and t % cs == 0
    seg_len = t if nhalf == 1 else r // nhalf
    pre_ln = ln_in is not None
    zero_init = states is None

    state_shapes = (
        (bsz, LANES, D_GROUP),
        (bsz, CONV_WIDTH - 1, SSD_CONV_DIM),
        (bsz, MLSTM_HEADS, MLSTM_DK, MLSTM_DV),
        (bsz, MLSTM_HEADS, MLSTM_DK),
        (bsz, MLSTM_HEADS, LANES),
        (bsz, 1, D_GROUP),
        (bsz, CONV_WIDTH - 1, D_GROUP),
        (bsz, GLA_HEADS // 2, GLA_DV, LANES),
    )

    def state_spec(shape):
        blk = (nb,) + shape[1:]
        nd = len(shape)
        return pl.BlockSpec(blk, lambda b, s, _nd=nd: (b,) + (0,) * (_nd - 1))

    x_spec = pl.BlockSpec((nb, t, D_MODEL), lambda b, s: (b, s, 0))
    args = [x]
    in_specs = [x_spec]
    if pre_ln:
        args += list(ln_in)
        in_specs += [_const_spec(a.shape) for a in ln_in]
    if not zero_init:
        args += list(states)
        in_specs += [state_spec(s) for s in state_shapes]
    args += list(w)
    in_specs += [_const_spec(a.shape) for a in w]

    out_shape = [jax.ShapeDtypeStruct(x.shape, F32)] + [jax.ShapeDtypeStruct(s, F32) for s in state_shapes]
    out_specs = [x_spec] + [state_spec(s) for s in state_shapes]

    vm = pltpu.VMEM
    scratch = [
        vm((r, D_MODEL), F32),
        vm((r, D_MODEL), BF16),
        vm((r, LANES), F32),
        vm((r, LANES), F32),
        vm((r, D_GROUP), F32),
        vm((r, D_GROUP), F32),
        vm((r, D_GROUP), BF16),
        vm((r, D_GROUP), BF16),
        vm((r, LANES), BF16),
        vm((r, LANES), BF16),
        vm((r, D_GROUP), F32),
        vm((r, D_GROUP), BF16),
        vm((r, D_GROUP), F32),
        vm((r, D_GROUP), BF16),
        vm((r, D_GROUP), F32),
        vm((r, GLA_HEADS * GLA_DK), BF16),
        vm((r, GLA_HEADS * GLA_DK), BF16),
        vm((r, GLA_HEADS * GLA_DK), BF16),
        vm((r, GLA_HEADS * GLA_DK), F32),
        vm((r, D_GROUP), BF16),
        vm((r, D_GROUP), F32),
        vm((r, D_GROUP), F32),
        vm((r, D_GROUP), F32),
        vm((r, D_MIX), BF16),
        vm((SUBLANES + seg_len, SSD_CONV_DIM), F32),
        vm((SUBLANES + seg_len, D_GROUP), F32),
        vm((nb, LANES, D_GROUP), F32),
        vm((nb, SUBLANES, SSD_CONV_DIM), F32),
        vm((nb, MLSTM_HEADS, MLSTM_DK, MLSTM_DV), F32),
        vm((nb, SUBLANES, MLSTM_DK), F32),
        vm((nb, SUBLANES, LANES), F32),
        vm((nb, 1, D_GROUP), F32),
        vm((nb, SUBLANES, D_GROUP), F32),
        vm((nb, GLA_HEADS // 2, GLA_DV, LANES), F32),
    ]

    kern = functools.partial(_mixer_kernel, nb=nb, t=t, cs=cs, nt=nt, nhalf=nhalf,
                             pre_ln=pre_ln, zero_init=zero_init)
    return pl.pallas_call(
        kern,
        grid=(bsz // nb, nt),
        in_specs=in_specs,
        out_specs=out_specs,
        out_shape=out_shape,
        scratch_shapes=scratch,
        compiler_params=pltpu.CompilerParams(
            dimension_semantics=("arbitrary", "arbitrary"),
            vmem_limit_bytes=VMEM_LIMIT_BYTES),
        name="mixer_layer",
    )(*args)


MLP_FF_CHUNK = 1024


def _mlp_kernel(x_ref, w1_ref, b1_ref, w2_ref, b2_ref, g_ref, b_ref, o_ref):
    x = x_ref[...]
    xb = x.astype(BF16)
    acc = jnp.zeros(x.shape, F32)
    for c in range(D_FF // MLP_FF_CHUNK):
        cols = slice(c * MLP_FF_CHUNK, (c + 1) * MLP_FF_CHUNK)
        hid = _dot(xb, w1_ref[:, cols]) + b1_ref[:, cols]
        hid = jnp.square(jnp.maximum(hid, 0.0))
        acc = acc + _dot(hid.astype(BF16), w2_ref[cols, :])
    y = DEEPNORM_ALPHA * x + acc + b2_ref[...]
    o_ref[...] = _layer_norm(y, g_ref[...], b_ref[...])


def _mlp_call(x2d, w1, b1, w2, b2, g, b, *, tm):
    n = x2d.shape[0]
    assert n % tm == 0
    row_spec = pl.BlockSpec((tm, D_MODEL), lambda i: (i, 0))

    def cspec(shape):
        nd = len(shape)
        return pl.BlockSpec(shape, lambda i, _nd=nd: (0,) * _nd, pipeline_mode=pl.Buffered(1))

    return pl.pallas_call(
        _mlp_kernel,
        grid=(n // tm,),
        in_specs=[row_spec] + [cspec(a.shape) for a in (w1, b1, w2, b2, g, b)],
        out_specs=row_spec,
        out_shape=jax.ShapeDtypeStruct(x2d.shape, F32),
        compiler_params=pltpu.CompilerParams(
            dimension_semantics=("arbitrary",), vmem_limit_bytes=VMEM_LIMIT_BYTES),
        name="mlp_layer",
    )(x2d, w1, b1, w2, b2, g, b)


def _pack_w_in(w_in):
    def sl(k):
        return w_in[..., _IN_OFFS[k]:_IN_OFFS[k + 1]]
    (z, xbc, dt, qm, km, vm_, ifm, om, xr, yr, qg, kg, vg, ag, gg) = [sl(k) for k in range(15)]
    pad = jnp.zeros(w_in.shape[:-1] + (LANES - SM_END,), w_in.dtype)
    small = jnp.concatenate([dt, ifm, ag, pad], axis=-1)
    packed = jnp.concatenate([z, xbc, qm, km, vm_, om, xr, yr, qg, kg, vg, gg, small], axis=-1)
    return packed.astype(BF16)


def _lane_row(parts, width=LANES):
    v = jnp.concatenate(parts, axis=-1)
    v = jnp.pad(v, ((0, 0), (0, width - v.shape[-1])))
    return v[:, None, :]


def _block_diag_pairs(wg):
    l = wg.shape[0]
    nblk = RGLRU_BLOCKS // 2
    w5 = wg.reshape(l, 2, nblk, RGLRU_BLOCK_DIM, RGLRU_BLOCK_DIM)
    eye = jnp.eye(nblk, dtype=wg.dtype)
    bd = jnp.einsum('lhnkj,nm->lhnkmj', w5, eye)
    return bd.reshape(l, 2, nblk * RGLRU_BLOCK_DIM, nblk * RGLRU_BLOCK_DIM).astype(BF16)


def _ssd_state_to_kernel(h):
    bsz = h.shape[0]
    ht = jnp.transpose(h, (0, 3, 1, 2)).reshape(bsz, SSD_STATE, D_GROUP)
    hpg = D_GROUP // SSD_GROUPS
    col_group = (jnp.arange(D_GROUP) // hpg)[None, None, :]
    parts = [jnp.where(col_group == g, ht, 0.0) for g in range(SSD_GROUPS)]
    return jnp.concatenate(parts, axis=1)


def _ssd_state_from_kernel(ht):
    bsz = ht.shape[0]
    hpg = D_GROUP // SSD_GROUPS
    h5 = ht.reshape(bsz, SSD_GROUPS, SSD_STATE, SSD_GROUPS, hpg)
    h = jnp.stack([h5[:, g, :, g] for g in range(SSD_GROUPS)], axis=2)
    return jnp.transpose(h.reshape(bsz, SSD_STATE, SSD_HEADS, SSD_HEAD_DIM), (0, 2, 3, 1))


def _gla_state_to_kernel(s):
    bsz = s.shape[0]
    s5 = s.reshape(bsz, GLA_HEADS // 2, 2, GLA_DK, GLA_DV)
    return jnp.transpose(s5, (0, 1, 4, 2, 3)).reshape(bsz, GLA_HEADS // 2, GLA_DV, 2 * GLA_DK)


def _gla_state_from_kernel(st):
    bsz = st.shape[0]
    s5 = st.reshape(bsz, GLA_HEADS // 2, GLA_DV, 2, GLA_DK)
    return jnp.transpose(s5, (0, 1, 3, 4, 2)).reshape(bsz, GLA_HEADS, GLA_DK, GLA_DV)


PROMPT_TILE = 512
MLP_TILE = 512
SAMPLE_SEQS_PER_STEP = 8


def kernel(x_prompt, x_sample, state_ssd_h, state_ssd_conv, state_mlstm_C, state_mlstm_n, state_mlstm_m, state_rglru_h, state_rglru_conv, state_gla_S, ln_in_g, ln_in_b, w_in, ssd_conv_w, ssd_conv_b, ssd_dt_bias, ssd_A_log, ssd_D, ssd_norm_w, mlstm_if_b, mlstm_norm_w, rg_conv_w, rg_conv_b, rg_gate_a_w, rg_gate_a_b, rg_gate_x_w, rg_gate_x_b, rg_lambda, gla_gate_w2, gla_gate_b, gla_norm_w, w_out, ln1_g, ln1_b, mlp_w1, mlp_b1, mlp_w2, mlp_b2, ln2_g, ln2_b):
    nl = w_in.shape[0]
    bp, lp, _ = x_prompt.shape
    bs, ls, _ = x_sample.shape

    def row(a):
        return a[:, None, :]

    w_in_p = _pack_w_in(w_in)
    sb_add = _lane_row([ssd_dt_bias, mlstm_if_b])
    alog = _lane_row([ssd_A_log])
    sd_exp = row(jnp.repeat(ssd_D, SSD_HEAD_DIM, axis=-1))
    rwa = _block_diag_pairs(rg_gate_a_w)
    rwx = _block_diag_pairs(rg_gate_x_w)
    gw2 = jnp.pad(gla_gate_w2, ((0, 0), (SM_AG, LANES - SM_END), (0, 0))).astype(BF16)
    w_out_b = w_out.astype(BF16)
    w1_b = mlp_w1.astype(BF16)
    w2_b = mlp_w2.astype(BF16)
    ln_in = (ln_in_g[None, :], ln_in_b[None, :])

    def layer_weights(l):
        return (w_in_p[l], sb_add[l], alog[l], ssd_conv_w[l], row(ssd_conv_b)[l], sd_exp[l],
                row(ssd_norm_w)[l], row(mlstm_norm_w)[l], rg_conv_w[l], row(rg_conv_b)[l],
                rwa[l], row(rg_gate_a_b)[l], rwx[l], row(rg_gate_x_b)[l], row(rg_lambda)[l],
                gw2[l], row(gla_gate_b)[l], row(gla_norm_w)[l], w_out_b[l], row(ln1_g)[l], row(ln1_b)[l])

    def mlp(x3, l, tm):
        shp = x3.shape
        y = _mlp_call(x3.reshape(-1, D_MODEL), w1_b[l], row(mlp_b1)[l], w2_b[l], row(mlp_b2)[l],
                      row(ln2_g)[l], row(ln2_b)[l], tm=tm)
        return y.reshape(shp)

    def finish_states(st, bsz):
        hT, sconv, c, n, m, rgh, rgconv, gst = st
        return (_ssd_state_from_kernel(hT), sconv, c, n, m[:, :, 0], rgh.reshape(bsz, D_GROUP), rgconv,
                _gla_state_from_kernel(gst))

    xp, xs = x_prompt, x_sample
    p_states, s_states = [], []
    for l in range(nl):
        w = layer_weights(l)
        first = ln_in if l == 0 else None
        outs = _mixer_call(xp, first, None, w, nb=1, t=PROMPT_TILE, cs=CHUNK, nhalf=2)
        xp = mlp(outs[0], l, MLP_TILE)
        p_states.append(finish_states(outs[1:], bp))

        cached = (
            _ssd_state_to_kernel(state_ssd_h[l]), state_ssd_conv[l], state_mlstm_C[l], state_mlstm_n[l],
            jnp.broadcast_to(state_mlstm_m[l][:, :, None], (bs, MLSTM_HEADS, LANES)),
            state_rglru_h[l][:, None, :], state_rglru_conv[l], _gla_state_to_kernel(state_gla_S[l]))
        cs_s = CHUNK if ls % CHUNK == 0 else ls
        outs = _mixer_call(xs, first, cached, w, nb=SAMPLE_SEQS_PER_STEP, t=ls, cs=cs_s, nhalf=1)
        xs = mlp(outs[0], l, bs * ls)
        s_states.append(finish_states(outs[1:], bs))

    p_stacked = [jnp.stack(v, axis=0) for v in zip(*p_states)]
    s_stacked = [jnp.stack(v, axis=0) for v in zip(*s_states)]
    return (xp, xs, *p_stacked, *s_stacked)
```

```python
import functools

import jax
import jax.numpy as jnp
from jax import lax
from jax.experimental import pallas as pl
from jax.experimental.pallas import tpu as pltpu

F32 = jnp.float32
BF16 = jnp.bfloat16

D_MODEL = 1024
DEPTH = 4
CHUNK = 64
CONV_WIDTH = 4
EPS = 1e-5
D_MIX = 2 * D_MODEL
D_GROUP = D_MIX // 4
SSD_HEAD_DIM = 64
SSD_HEADS = D_GROUP // SSD_HEAD_DIM
SSD_GROUPS = 2
SSD_STATE = 64
SSD_CONV_DIM = D_GROUP + 2 * SSD_GROUPS * SSD_STATE
MLSTM_HEADS = 4
MLSTM_DK = D_GROUP // MLSTM_HEADS
MLSTM_DV = D_GROUP // MLSTM_HEADS
RGLRU_BLOCKS = 8
RGLRU_BLOCK_DIM = D_GROUP // RGLRU_BLOCKS
RGLRU_C = 8.0
GLA_HEADS = 4
GLA_DV = D_GROUP // GLA_HEADS
GLA_DK = GLA_DV // 2
GLA_GATE_RANK = 16
GLA_TAU = 16.0
D_FF = 4 * D_MODEL
DEEPNORM_ALPHA = (2 * DEPTH) ** 0.25

IN_SPLIT_SIZES = (
    D_GROUP, SSD_CONV_DIM, SSD_HEADS,
    MLSTM_HEADS * MLSTM_DK, MLSTM_HEADS * MLSTM_DK, MLSTM_HEADS * MLSTM_DV, 2 * MLSTM_HEADS,
    MLSTM_HEADS * MLSTM_DV, D_GROUP, D_GROUP,
    GLA_HEADS * GLA_DK, GLA_HEADS * GLA_DK, GLA_HEADS * GLA_DV, GLA_GATE_RANK, GLA_HEADS * GLA_DV,
)
_IN_OFFS = [0]
for _s in IN_SPLIT_SIZES:
    _IN_OFFS.append(_IN_OFFS[-1] + _s)

LANES = 128
SUBLANES = 8
VMEM_LIMIT_BYTES = 56 * 1024 * 1024

OFF_Z = 0
OFF_XBC = OFF_Z + D_GROUP
OFF_QM = OFF_XBC + SSD_CONV_DIM
OFF_KM = OFF_QM + D_GROUP
OFF_VM = OFF_KM + D_GROUP
OFF_OM = OFF_VM + D_GROUP
OFF_XR = OFF_OM + D_GROUP
OFF_YR = OFF_XR + D_GROUP
OFF_QG = OFF_YR + D_GROUP
OFF_KG = OFF_QG + GLA_HEADS * GLA_DK
OFF_VG = OFF_KG + GLA_HEADS * GLA_DK
OFF_GG = OFF_VG + D_GROUP
OFF_SMALL = OFF_GG + D_GROUP
D_IN_PACKED = OFF_SMALL + LANES
SM_DT = 0
SM_I = SM_DT + SSD_HEADS
SM_F = SM_I + MLSTM_HEADS
SM_AG = SM_F + MLSTM_HEADS
SM_END = SM_AG + GLA_GATE_RANK

MIX_SSD = 0
MIX_ML = D_GROUP
MIX_RG = 2 * D_GROUP
MIX_GLA = 3 * D_GROUP

OUT_PROJ_BLOCKS = 4
RG_BLOCKS_PER_STAGE = 4
CONV_COL_BLOCK = 256


def _dot(a, b):
    return jnp.dot(a, b, preferred_element_type=F32)


def _dot_nt(a, b):
    return lax.dot_general(a, b, (((1,), (1,)), ((), ())), preferred_element_type=F32)


def _dot_tn(a, b):
    return lax.dot_general(a, b, (((0,), (0,)), ((), ())), preferred_element_type=F32)


def _split3(v):
    hi = v.astype(BF16)
    r = v - hi.astype(F32)
    mid = r.astype(BF16)
    lo = (r - mid.astype(F32)).astype(BF16)
    return hi, mid, lo


def _sel_left(m01, v):
    hi, mid, lo = _split3(v)
    return _dot(m01, hi) + _dot(m01, mid) + _dot(m01, lo)


def _sel_right(v, m01):
    hi, mid, lo = _split3(v)
    return _dot(hi, m01) + _dot(mid, m01) + _dot(lo, m01)


def _layer_norm(x, g, b):
    mu = jnp.mean(x, -1, keepdims=True)
    xc = x - mu
    var = jnp.mean(xc * xc, -1, keepdims=True)
    return xc * lax.rsqrt(var + EPS) * g + b


def _chunk_last(v, cs):
    r, c = v.shape
    v3 = v.reshape(r // cs, cs, c)
    last = v3[:, cs - 1:cs, :]
    return jnp.broadcast_to(last, (r // cs, cs, c)).reshape(r, c)


def _run_interleaved(gens):
    live = [[g, 0, n] for g, n in gens]
    while live:
        entry = min(live, key=lambda e: (e[1] + 1) / e[2])
        next(entry[0])
        entry[1] += 1
        if entry[1] == entry[2]:
            assert next(entry[0], live) is live, "generator has more stages than declared"
            live.remove(entry)


def _mixer_kernel(*refs, nb, t, cs, nt, nhalf, pre_ln, zero_init, pipelined, tiles_per_seq, n_steps):
    r = nb * t
    rh = r // nhalf
    it = iter(refs)
    x_ref = next(it)
    if pre_ln:
        lng_ref, lnb_ref = next(it), next(it)
    if not zero_init:
        (i_hT, i_sconv, i_C, i_n, i_m, i_rgh, i_rgconv, i_ST) = [next(it) for _ in range(8)]
    (w_in_ref, sbadd_ref, alog_ref, sconvw_ref, sconvb_ref, sD_ref, snw_ref, mnw_ref,
     rconvw_ref, rconvb_ref, rwa_ref, rba_ref, rwx_ref, rbx_ref, rlam_ref,
     gw2_ref, ggb_ref, gnw_ref, w_out_ref, ln1g_ref, ln1b_ref) = [next(it) for _ in range(21)]
    (o_x, o_hT, o_sconv, o_C, o_n, o_m, o_rgh, o_rgconv, o_ST) = [next(it) for _ in range(9)]
    (xn_ref, xb_ref, sp_ref, cum_ref, z_ref, xsf_ref, xsb_ref, xw_ref, bm_ref, cm_ref, ea_ref,
     q_ref, ks_ref, v_ref, o_ref, qt_ref, kt_ref, kd_ref, dch_ref, vg_ref, gg_ref,
     ra_ref, rb_ref, mix_ref, rgo_ref, xo_ref, sdch_ref, ubs_ref, ubr_ref,
     s_hT, s_sconv, s_C, s_n, s_m, s_rgh, s_rgconv, s_ST) = [next(it) for _ in range(37)]
    tile_bufs = (xn_ref, xb_ref, sp_ref, cum_ref, z_ref, xsf_ref, xsb_ref, xw_ref, bm_ref, cm_ref, ea_ref,
                 q_ref, ks_ref, v_ref, o_ref, qt_ref, kt_ref, kd_ref, dch_ref, vg_ref, gg_ref,
                 ra_ref, rb_ref, mix_ref, rgo_ref, xo_ref, sdch_ref)
    prep_states = (s_sconv, s_rgh, s_rgconv)
    mix_states = (s_hT, s_C, s_n, s_m, s_ST)

    def zero_all(bufs):
        for ref in bufs:
            ref[...] = jnp.zeros(ref.shape, ref.dtype)

    if pipelined:
        step = pl.program_id(0)
        first_tile = 2 * step
        seq_starts = first_tile % tiles_per_seq == 0

        @pl.when(step == 0)
        def _():
            zero_all(tile_bufs + mix_states)

        @pl.when(seq_starts)
        def _():
            zero_all(prep_states)
    else:
        ti = pl.program_id(1)

        @pl.when(ti == 0)
        def _():
            if zero_init:
                zero_all(prep_states + mix_states)
            else:
                s_hT[...] = i_hT[...]
                s_sconv[...] = jnp.zeros_like(s_sconv)
                s_sconv[:, SUBLANES - 3:SUBLANES, :] = i_sconv[...]
                s_C[...] = i_C[...]
                s_n[...] = jnp.zeros_like(s_n)
                s_n[:, 0:MLSTM_HEADS, :] = i_n[...]
                s_m[...] = jnp.zeros_like(s_m)
                s_m[:, 0:MLSTM_HEADS, :] = i_m[...]
                s_rgh[...] = i_rgh[...]
                s_rgconv[...] = jnp.zeros_like(s_rgconv)
                s_rgconv[:, SUBLANES - 3:SUBLANES, :] = i_rgconv[...]
                s_ST[...] = i_ST[...]

    lane = lax.broadcasted_iota(jnp.int32, (1, LANES), 1)
    lane_lo = lane < (LANES // 2)
    head_lane = lane < SM_I
    rr = lax.broadcasted_iota(jnp.int32, (rh, rh), 0)
    cc = lax.broadcasted_iota(jnp.int32, (rh, rh), 1)
    bdtri = jnp.where((rr // cs == cc // cs) & (cc <= rr), 1.0, 0.0).astype(BF16)
    er = lax.broadcasted_iota(jnp.int32, (LANES, D_GROUP), 0)
    ec = lax.broadcasted_iota(jnp.int32, (LANES, D_GROUP), 1)
    expand = jnp.where(ec // SSD_HEAD_DIM == er, 1.0, 0.0).astype(BF16)
    ssd_blockmask = (er // SSD_STATE) == (ec // (D_GROUP // SSD_GROUPS))
    trow = lax.broadcasted_iota(jnp.int32, (cs, cs), 0)
    tcol = lax.broadcasted_iota(jnp.int32, (cs, cs), 1)
    tri = tcol <= trow
    neg_inf = -jnp.inf
    inv_dv = 1.0 / MLSTM_DV

    def segments(hf):
        if nhalf == 1:
            return [(b, b * t, t) for b in range(nb)]
        return [(0, hf * rh, rh)]

    def load_x(hf):
        if nb == 1:
            return x_ref[0, hf * rh:(hf + 1) * rh, :]
        return x_ref[...].reshape(r, D_MODEL)

    def store_out(hf, cols, val):
        if nb == 1:
            o_x[0, hf * rh:(hf + 1) * rh, cols] = val
        else:
            o_x[:, :, cols] = val.reshape(nb, t, val.shape[-1])

    def load_out(hf):
        if nb == 1:
            return o_x[0, hf * rh:(hf + 1) * rh, :]
        return o_x[...].reshape(r, D_MODEL)

    def prep_recurrent(hf):
        h0 = hf * rh
        hr = slice(h0, h0 + rh)
        x = load_x(hf)
        if pre_ln:
            x = _layer_norm(x, lng_ref[...], lnb_ref[...])
        xn_ref[hr, :] = x
        xb_ref[hr, :] = x.astype(BF16)
        yield

        def seg(off, width):
            return _dot(xb_ref[hr, :], w_in_ref[:, off:off + width])

        for cblk in range(SSD_CONV_DIM // CONV_COL_BLOCK):
            cols = slice(cblk * CONV_COL_BLOCK, (cblk + 1) * CONV_COL_BLOCK)
            xbc_raw = seg(OFF_XBC + cblk * CONV_COL_BLOCK, CONV_COL_BLOCK)
            cw = sconvw_ref[:, cols]
            cb = sconvb_ref[:, cols]
            for (b, s0, sl) in segments(hf):
                rows = slice(s0, s0 + sl)
                ubs_ref[0:SUBLANES, cols] = s_sconv[b, :, cols]
                ubs_ref[SUBLANES:SUBLANES + sl, cols] = xbc_raw[s0 - h0:s0 - h0 + sl]
                acc = cb + ubs_ref[5:5 + sl, cols] * cw[0:1]
                for j in range(1, CONV_WIDTH):
                    acc = acc + ubs_ref[5 + j:5 + j + sl, cols] * cw[j:j + 1]
                s_sconv[b, SUBLANES - 3:SUBLANES, cols] = ubs_ref[5 + sl:8 + sl, cols]
                xbc = jax.nn.silu(acc)
                if cblk < D_GROUP // CONV_COL_BLOCK:
                    xsf_ref[rows, cols] = xbc
                    xsb_ref[rows, cols] = xbc.astype(BF16)
                else:
                    bm_ref[rows, :] = xbc[:, 0:LANES].astype(BF16)
                    cm_ref[rows, :] = xbc[:, LANES:2 * LANES].astype(BF16)
            yield

        s_raw = seg(OFF_SMALL, LANES)
        sb = s_raw + sbadd_ref[...]
        dt = jax.nn.softplus(sb)
        logf = jax.nn.log_sigmoid(sb)
        sp = jnp.where(lane < SM_I, dt, jnp.where(lane < SM_F, sb, jnp.where(lane < SM_AG, logf, 0.0)))
        sp_ref[hr, :] = sp
        cum = _sel_left(bdtri, sp)
        a_neg = -jnp.exp(alog_ref[...])
        cuma = jnp.where(head_lane, cum * a_neg, cum)
        cum_ref[hr, :] = cuma
        acum_last = _chunk_last(cuma, cs)
        ea8 = jnp.where(head_lane, jnp.exp(cuma), 0.0)
        wl8 = jnp.where(head_lane, jnp.exp(acum_last - cuma) * sp, 0.0)
        yield
        ea_ref[hr, :] = _dot(ea8.astype(BF16), expand)
        wl_exp = _dot(wl8.astype(BF16), expand)
        dl8 = jnp.where(head_lane, jnp.exp(acum_last), 0.0).reshape(rh // cs, cs, LANES)[:, 0:SUBLANES, :]
        nrow = (rh // cs) * SUBLANES
        sdch_ref[hf * nrow:(hf + 1) * nrow, :] = _sel_right(dl8.reshape(nrow, LANES), expand)
        xw_ref[hr, :] = (xsf_ref[hr, :] * wl_exp).astype(BF16)
        yield

        for cblk in range(D_GROUP // CONV_COL_BLOCK):
            cols = slice(cblk * CONV_COL_BLOCK, (cblk + 1) * CONV_COL_BLOCK)
            xr_raw = seg(OFF_XR + cblk * CONV_COL_BLOCK, CONV_COL_BLOCK)
            rcw = rconvw_ref[:, cols]
            rcb = rconvb_ref[:, cols]
            log_sig_lam = jax.nn.log_sigmoid(rlam_ref[:, cols])
            for (b, s0, sl) in segments(hf):
                rows = slice(s0, s0 + sl)
                ubr_ref[0:SUBLANES, cols] = s_rgconv[b, :, cols]
                ubr_ref[SUBLANES:SUBLANES + sl, cols] = xr_raw[s0 - h0:s0 - h0 + sl]
                xr = rcb + ubr_ref[5:5 + sl, cols] * rcw[0:1]
                for j in range(1, CONV_WIDTH):
                    xr = xr + ubr_ref[5 + j:5 + j + sl, cols] * rcw[j:j + 1]
                s_rgconv[b, SUBLANES - 3:SUBLANES, cols] = ubr_ref[5 + sl:8 + sl, cols]
                xrb = xr.astype(BF16)
                rgate = jax.nn.sigmoid(_dot(xrb, rwa_ref[cblk]) + rba_ref[:, cols])
                igate = jax.nn.sigmoid(_dot(xrb, rwx_ref[cblk]) + rbx_ref[:, cols])
                log_a = RGLRU_C * rgate * log_sig_lam
                a_rg = jnp.exp(log_a)
                ra_ref[rows, cols] = a_rg
                rb_ref[rows, cols] = jnp.sqrt(-jnp.tanh(log_a) * (a_rg * a_rg + 1.0)) * (igate * xr)
            yield
        srow = lax.broadcasted_iota(jnp.int32, (SUBLANES, D_GROUP), 0)
        for (b, s0, sl) in segments(hf):
            h = s_rgh[b]
            for jb in range(sl // SUBLANES):
                rs = slice(s0 + jb * SUBLANES, s0 + (jb + 1) * SUBLANES)
                a = ra_ref[rs, :]
                bb = rb_ref[rs, :]
                for d in (1, 2, 4):
                    a_sh = pltpu.roll(a, d, 0)
                    b_sh = pltpu.roll(bb, d, 0)
                    ok = srow >= d
                    bb = bb + a * jnp.where(ok, b_sh, 0.0)
                    a = a * jnp.where(ok, a_sh, 1.0)
                hb = bb + a * h
                rb_ref[rs, :] = hb
                h = hb[SUBLANES - 1:SUBLANES, :]
                if jb % RG_BLOCKS_PER_STAGE == RG_BLOCKS_PER_STAGE - 1:
                    yield
            s_rgh[b] = h
        yr = seg(OFF_YR, D_GROUP)
        rgo_ref[hr, :] = (rb_ref[hr, :] * jax.nn.gelu(yr)).astype(BF16)
        yield

    def prep_project(hf):
        hr = slice(hf * rh, (hf + 1) * rh)

        def seg(off, width):
            return _dot(xb_ref[hr, :], w_in_ref[:, off:off + width])

        z_ref[hr, :] = seg(OFF_Z, D_GROUP)
        yield
        vg_ref[hr, :] = seg(OFF_VG, D_GROUP).astype(BF16)
        yield
        s_raw = seg(OFF_SMALL, LANES)
        la = jax.nn.log_sigmoid(_dot(s_raw.astype(BF16), gw2_ref[...]) + ggb_ref[...]) * (1.0 / GLA_TAU)
        bcum = _sel_left(bdtri, la)
        bl = _chunk_last(bcum, cs)
        dch_ref[hr, :] = jnp.exp(bl)
        yield
        gg_ref[hr, :] = seg(OFF_GG, D_GROUP)
        yield
        qg = seg(OFF_QG, GLA_HEADS * GLA_DK) * (GLA_DK ** -0.5)
        qt_ref[hr, :] = (qg * jnp.exp(bcum)).astype(BF16)
        kg = seg(OFF_KG, GLA_HEADS * GLA_DK)
        kt_ref[hr, :] = (kg * jnp.exp(-bcum)).astype(BF16)
        kd_ref[hr, :] = (kg * jnp.exp(bl - bcum)).astype(BF16)
        yield
        q_ref[hr, :] = seg(OFF_QM, D_GROUP).astype(BF16)
        yield
        ks_ref[hr, :] = seg(OFF_KM, D_GROUP) * (MLSTM_DK ** -0.5)
        yield
        v_ref[hr, :] = seg(OFF_VM, D_GROUP).astype(BF16)
        yield
        o_ref[hr, :] = seg(OFF_OM, D_GROUP)
        yield

    def chunk_plan(hf):
        ids = list(range(hf * rh // cs, (hf + 1) * rh // cs))
        rows_of = {i: slice(i * cs, (i + 1) * cs) for i in ids}
        last_rows_of = {i: slice((i + 1) * cs - SUBLANES, (i + 1) * cs) for i in ids}
        seq_of = {i: (i * cs) // t for i in ids}
        load_at = {i: (i == ids[0]) or ((i * cs) % t == 0) for i in ids}
        store_at = {i: (i == ids[-1]) or (((i + 1) * cs) % t == 0) for i in ids}
        return ids, rows_of, last_rows_of, seq_of, load_at, store_at

    def mix_gens(hf):
        ids, rows_of, last_rows_of, seq_of, load_at, store_at = chunk_plan(hf)
        shared = {}

        def transposed():
            if not shared:
                shared['sp_c'] = {i: sp_ref[rows_of[i], :] for i in ids}
                shared['cum_c'] = {i: cum_ref[rows_of[i], :] for i in ids}
                shared['sp_t'] = {i: shared['sp_c'][i].T for i in ids}
                shared['cum_t'] = {i: shared['cum_c'][i].T for i in ids}
            return shared['sp_c'], shared['cum_c'], shared['sp_t'], shared['cum_t']

        def ssd():
            sp_c, cum_c, sp_t, cum_t = transposed()
            cm_c = {i: cm_ref[rows_of[i], :] for i in ids}
            bm_c = {i: bm_ref[rows_of[i], :] for i in ids}
            zero_b = jnp.zeros((cs, LANES), BF16)
            g_mat = {i: [_dot_nt(jnp.where(lane_lo, cm_c[i], zero_b), bm_c[i]),
                         _dot_nt(jnp.where(lane_lo, zero_b, cm_c[i]), bm_c[i])] for i in ids}
            upd = {i: _dot_tn(bm_c[i], xw_ref[rows_of[i], :]) for i in ids}
            yield
            m_ssd = {}
            for i in ids:
                per_head = []
                for h in range(SSD_HEADS):
                    col = cum_c[i][:, SM_DT + h:SM_DT + h + 1]
                    row = cum_t[i][SM_DT + h:SM_DT + h + 1, :]
                    dtrow = sp_t[i][SM_DT + h:SM_DT + h + 1, :]
                    decay = jnp.exp(jnp.where(tri, col - row, neg_inf))
                    per_head.append((g_mat[i][h // (SSD_HEADS // SSD_GROUPS)] * decay * dtrow).astype(BF16))
                m_ssd[i] = per_head
                if i % 2 == 1:
                    yield
            y_intra = {}
            for i in ids:
                xs_b = xsb_ref[rows_of[i], :]
                pairs = []
                for hp in range(SSD_HEADS // 2):
                    blk = xs_b[:, hp * LANES:(hp + 1) * LANES]
                    pairs.append(jnp.where(lane_lo, _dot(m_ssd[i][2 * hp], blk), _dot(m_ssd[i][2 * hp + 1], blk)))
                y_intra[i] = jnp.concatenate(pairs, axis=1)
            yield
            h_in = {}
            h_t = None
            for i in ids:
                if load_at[i]:
                    h_t = s_hT[seq_of[i]]
                h_in[i] = h_t
                dch = sdch_ref[i * SUBLANES:i * SUBLANES + 1, :]
                h_t = dch * h_t + jnp.where(ssd_blockmask, upd[i], 0.0)
                if store_at[i]:
                    s_hT[seq_of[i]] = h_t
            yield
            for i in ids:
                rs = rows_of[i]
                y = _dot(cm_c[i], h_in[i].astype(BF16)) * ea_ref[rs, :]
                y = y + y_intra[i] + sD_ref[...] * xsf_ref[rs, :]
                yz = y * jax.nn.silu(z_ref[rs, :])
                yz = yz * lax.rsqrt(jnp.mean(yz * yz, -1, keepdims=True) + EPS) * snw_ref[...]
                mix_ref[rs, MIX_SSD:MIX_SSD + D_GROUP] = yz.astype(BF16)
                if i % 2 == 1:
                    yield

        def mlstm(h):
            sp_c, cum_c, sp_t, cum_t = transposed()
            hl = slice(h * MLSTM_DK, (h + 1) * MLSTM_DK)
            q_h = {i: q_ref[rows_of[i], hl] for i in ids}
            v_h = {i: v_ref[rows_of[i], hl] for i in ids}
            k_h = {i: ks_ref[rows_of[i], hl] for i in ids}
            bcol = {i: cum_c[i][:, SM_F + h:SM_F + h + 1] for i in ids}
            dmat, m_intra, b_last, m_loc, kw = {}, {}, {}, {}, {}
            for i in ids:
                brow = cum_t[i][SM_F + h:SM_F + h + 1, :]
                icol = sp_c[i][:, SM_I + h:SM_I + h + 1]
                irow = sp_t[i][SM_I + h:SM_I + h + 1, :]
                dm = jnp.where(tri, bcol[i] - brow + irow, neg_inf)
                dmat[i] = dm
                m_intra[i] = jnp.max(dm, axis=1, keepdims=True)
                bl1 = bcol[i][cs - 1:cs, :]
                b_last[i] = bl1
                lw = bl1 - bcol[i] + icol
                ml = jnp.max(lw, axis=0, keepdims=True)
                m_loc[i] = ml
                kw[i] = k_h[i] * jnp.exp(lw - ml)
            qk_raw = {i: _dot_nt(q_h[i], k_h[i].astype(BF16)) for i in ids}
            c_chunk = {i: _dot_tn(kw[i].astype(BF16), v_h[i]) for i in ids}
            n_chunk = {i: jnp.sum(kw[i], axis=0, keepdims=True) for i in ids}
            yield
            qk = {i: qk_raw[i] * jnp.exp(dmat[i] - m_intra[i]) for i in ids}
            num_intra = {i: _dot(qk[i].astype(BF16), v_h[i]) for i in ids}
            den_intra = {i: jnp.sum(qk[i], axis=1, keepdims=True) for i in ids}
            yield
            c_in, n_in, m_in = {}, {}, {}
            c_s = n_s = m_s = None
            for i in ids:
                sq = seq_of[i]
                if load_at[i]:
                    c_s = s_C[sq, h]
                    n_s = s_n[sq, h:h + 1, :]
                    m_s = s_m[sq, h:h + 1, :][:, 0:1]
                c_in[i], n_in[i], m_in[i] = c_s, n_s, m_s
                m_new = jnp.maximum(b_last[i] + m_s, m_loc[i])
                a_old = jnp.exp(b_last[i] + m_s - m_new)
                a_new = jnp.exp(m_loc[i] - m_new)
                c_s = a_old * c_s + a_new * c_chunk[i]
                n_s = a_old * n_s + a_new * n_chunk[i]
                m_s = m_new
                if store_at[i]:
                    s_C[sq, h] = c_s
                    s_n[sq, h:h + 1, :] = n_s
                    s_m[sq, h:h + 1, :] = jnp.broadcast_to(m_s, (1, LANES))
            yield
            inter = {i: _dot(q_h[i], c_in[i].astype(BF16)) for i in ids}
            qn = {i: jnp.sum(q_h[i].astype(F32) * n_in[i], axis=1, keepdims=True) for i in ids}
            yield
            for i in ids:
                rs = rows_of[i]
                g = bcol[i] + m_in[i]
                m_t = jnp.maximum(g, m_intra[i])
                a_inter = jnp.exp(g - m_t)
                a_intra = jnp.exp(m_intra[i] - m_t)
                num = a_inter * inter[i] + a_intra * num_intra[i]
                den = a_inter * qn[i] + a_intra * den_intra[i]
                hout = num / jnp.maximum(jnp.abs(den), jnp.exp(-m_t))
                hn = hout * lax.rsqrt(jnp.sum(hout * hout, -1, keepdims=True) * inv_dv + EPS)
                ym = jax.nn.sigmoid(o_ref[rs, hl]) * (hn * mnw_ref[:, hl])
                mix_ref[rs, MIX_ML + h * MLSTM_DV:MIX_ML + (h + 1) * MLSTM_DV] = ym.astype(BF16)
            yield

        def gla(j):
            bl_ = slice(j * LANES, (j + 1) * LANES)
            zq = jnp.zeros((cs, LANES), BF16)
            qt = {i: qt_ref[rows_of[i], bl_] for i in ids}
            kt = {i: kt_ref[rows_of[i], bl_] for i in ids}
            kd = {i: kd_ref[rows_of[i], bl_] for i in ids}
            qm = {i: [jnp.where(lane_lo, qt[i], zq), jnp.where(lane_lo, zq, qt[i])] for i in ids}
            v_h = {i: [vg_ref[rows_of[i], (2 * j + h2) * GLA_DV:(2 * j + h2 + 1) * GLA_DV] for h2 in range(2)]
                   for i in ids}
            att_raw = {i: [_dot_nt(qm[i][h2], kt[i]) for h2 in range(2)] for i in ids}
            prods = {i: [_dot_tn(v_h[i][h2], kd[i]) for h2 in range(2)] for i in ids}
            yield
            att = {i: [jnp.where(tri, att_raw[i][h2], 0.0).astype(BF16) for h2 in range(2)] for i in ids}
            o_intra = {i: [_dot(att[i][h2], v_h[i][h2]) for h2 in range(2)] for i in ids}
            yield
            st_in = {}
            st = None
            for i in ids:
                sq = seq_of[i]
                if load_at[i]:
                    st = s_ST[sq, j]
                st_in[i] = st
                dg = dch_ref[last_rows_of[i], bl_][SUBLANES - 1:SUBLANES, :]
                st = dg * st + jnp.where(lane_lo, prods[i][0], prods[i][1])
                if store_at[i]:
                    s_ST[sq, j] = st
            yield
            o_inter = {i: [_dot_nt(qm[i][h2], st_in[i].astype(BF16)) for h2 in range(2)] for i in ids}
            yield
            for i in ids:
                rs = rows_of[i]
                for h2 in range(2):
                    h = 2 * j + h2
                    hv = slice(h * GLA_DV, (h + 1) * GLA_DV)
                    o = o_intra[i][h2] + o_inter[i][h2]
                    on = o * lax.rsqrt(jnp.sum(o * o, -1, keepdims=True) * (1.0 / GLA_DV) + EPS)
                    yg = on * gnw_ref[:, hv] * jax.nn.silu(gg_ref[rs, hv])
                    mix_ref[rs, MIX_GLA + h * GLA_DV:MIX_GLA + (h + 1) * GLA_DV] = yg.astype(BF16)
            yield

        def handoff():
            hr = slice(hf * rh, (hf + 1) * rh)
            mix_ref[hr, MIX_RG:MIX_RG + D_GROUP] = rgo_ref[hr, :]
            yield
            xo_ref[hr, :] = xn_ref[hr, :]
            yield

        n_odd = sum(1 for i in ids if i % 2 == 1)
        return ([(ssd(), 3 + 2 * n_odd)] + [(mlstm(h), 5) for h in range(MLSTM_HEADS)]
                + [(gla(j), 5) for j in range(GLA_HEADS // 2)] + [(handoff(), 2)])

    def prep_gens(hf):
        n_scan = sum((sl // SUBLANES) // RG_BLOCKS_PER_STAGE for (_, _, sl) in segments(hf))
        n_recurrent = 1 + SSD_CONV_DIM // CONV_COL_BLOCK + 2 + D_GROUP // CONV_COL_BLOCK + n_scan + 1
        return [(prep_recurrent(hf), n_recurrent), (prep_project(hf), 9)]

    out_stages = OUT_PROJ_BLOCKS + 1

    def outp(hf):
        hr = slice(hf * rh, (hf + 1) * rh)
        wblk = D_MODEL // OUT_PROJ_BLOCKS
        for c in range(OUT_PROJ_BLOCKS):
            cols = slice(c * wblk, (c + 1) * wblk)
            y1 = DEEPNORM_ALPHA * xo_ref[hr, cols] + _dot(mix_ref[hr, :], w_out_ref[:, cols])
            store_out(hf, cols, y1)
            yield
        store_out(hf, slice(0, D_MODEL), _layer_norm(load_out(hf), ln1g_ref[...], ln1b_ref[...]))
        yield

    def write_mix_states():
        o_hT[...] = s_hT[...]
        o_C[...] = s_C[...]
        o_n[...] = s_n[:, 0:MLSTM_HEADS, :]
        o_m[...] = s_m[:, 0:MLSTM_HEADS, :]
        o_ST[...] = s_ST[...]

    def write_prep_states():
        o_sconv[...] = s_sconv[:, SUBLANES - 3:SUBLANES, :]
        o_rgh[...] = s_rgh[...]
        o_rgconv[...] = s_rgconv[:, SUBLANES - 3:SUBLANES, :]

    if pipelined:
        def phase(prep_half, mix_half):
            _run_interleaved(prep_gens(prep_half) + mix_gens(mix_half) + [(outp(prep_half), out_stages)])

        phase(0, 1)

        @pl.when(seq_starts & (step > 0))
        def _():
            write_mix_states()

        @pl.when(seq_starts)
        def _():
            zero_all(mix_states)

        phase(1, 0)

        @pl.when(((first_tile + 2) % tiles_per_seq == 0) & (step < n_steps - 1))
        def _():
            write_prep_states()
    else:
        _run_interleaved(prep_gens(0))
        _run_interleaved(mix_gens(0))
        _run_interleaved([(outp(0), out_stages)])

        @pl.when(ti == nt - 1)
        def _():
            write_mix_states()
            write_prep_states()


def _const_spec(shape):
    nd = len(shape)
    return pl.BlockSpec(shape, lambda *_, _nd=nd: (0,) * _nd, pipeline_mode=pl.Buffered(1))


def _mixer_call(x, ln_in, states, w, *, nb, t, cs, pipelined):
    bsz, seq, _ = x.shape
    assert bsz % nb == 0 and seq % t == 0 and t % SUBLANES == 0
    nhalf = 2 if pipelined else 1
    assert not pipelined or (nb == 1 and states is None)
    nt = seq // t
    r = nb * t
    assert r % nhalf == 0 and (r // nhalf) % cs == 0 and t % cs == 0
    seg_len = t if nhalf == 1 else r // nhalf
    pre_ln = ln_in is not None
    zero_init = states is None
    tiles_per_seq = nhalf * nt
    n_blocks = bsz * nt
    n_steps = n_blocks + 1 if pipelined else None

    state_shapes = (
        (bsz, LANES, D_GROUP),
        (bsz, CONV_WIDTH - 1, SSD_CONV_DIM),
        (bsz, MLSTM_HEADS, MLSTM_DK, MLSTM_DV),
        (bsz, MLSTM_HEADS, MLSTM_DK),
        (bsz, MLSTM_HEADS, LANES),
        (bsz, 1, D_GROUP),
        (bsz, CONV_WIDTH - 1, D_GROUP),
        (bsz, GLA_HEADS // 2, GLA_DV, LANES),
    )

    prep_state_ids = (1, 5, 6)

    if pipelined:
        n_tiles = nhalf * n_blocks

        def state_spec(shape, idx):
            nd = len(shape)
            if idx in prep_state_ids:
                return pl.BlockSpec((1,) + shape[1:], lambda s, _nd=nd: (
                    jnp.minimum(2 * s + 1, n_tiles - 1) // tiles_per_seq,) + (0,) * (_nd - 1))
            return pl.BlockSpec((1,) + shape[1:], lambda s, _nd=nd: (
                jnp.maximum(2 * s - 1, 0) // tiles_per_seq,) + (0,) * (_nd - 1))

        x = x.reshape(n_blocks, t, D_MODEL)
        x_spec = pl.BlockSpec((1, t, D_MODEL), lambda s: (jnp.minimum(s, n_blocks - 1), 0, 0))
        out_x_spec = pl.BlockSpec((1, t, D_MODEL), lambda s: (jnp.maximum(s - 1, 0), 0, 0))
        grid = (n_steps,)
    else:
        def state_spec(shape, idx):
            blk = (nb,) + shape[1:]
            nd = len(shape)
            return pl.BlockSpec(blk, lambda b, s, _nd=nd: (b,) + (0,) * (_nd - 1))

        x_spec = pl.BlockSpec((nb, t, D_MODEL), lambda b, s: (b, s, 0))
        out_x_spec = x_spec
        grid = (bsz // nb, nt)
    args = [x]
    in_specs = [x_spec]
    if pre_ln:
        args += list(ln_in)
        in_specs += [_const_spec(a.shape) for a in ln_in]
    if not zero_init:
        args += list(states)
        in_specs += [state_spec(s, i) for i, s in enumerate(state_shapes)]
    args += list(w)
    in_specs += [_const_spec(a.shape) for a in w]

    out_shape = [jax.ShapeDtypeStruct(x.shape, F32)] + [jax.ShapeDtypeStruct(s, F32) for s in state_shapes]
    out_specs = [out_x_spec] + [state_spec(s, i) for i, s in enumerate(state_shapes)]

    vm = pltpu.VMEM
    scratch = [
        vm((r, D_MODEL), F32),
        vm((r, D_MODEL), BF16),
        vm((r, LANES), F32),
        vm((r, LANES), F32),
        vm((r, D_GROUP), F32),
        vm((r, D_GROUP), F32),
        vm((r, D_GROUP), BF16),
        vm((r, D_GROUP), BF16),
        vm((r, LANES), BF16),
        vm((r, LANES), BF16),
        vm((r, D_GROUP), F32),
        vm((r, D_GROUP), BF16),
        vm((r, D_GROUP), F32),
        vm((r, D_GROUP), BF16),
        vm((r, D_GROUP), F32),
        vm((r, GLA_HEADS * GLA_DK), BF16),
        vm((r, GLA_HEADS * GLA_DK), BF16),
        vm((r, GLA_HEADS * GLA_DK), BF16),
        vm((r, GLA_HEADS * GLA_DK), F32),
        vm((r, D_GROUP), BF16),
        vm((r, D_GROUP), F32),
        vm((r, D_GROUP), F32),
        vm((r, D_GROUP), F32),
        vm((r, D_MIX), BF16),
        vm((r, D_GROUP), BF16),
        vm((r, D_MODEL), F32),
        vm((r // cs * SUBLANES, D_GROUP), F32),
        vm((SUBLANES + seg_len, SSD_CONV_DIM), F32),
        vm((SUBLANES + seg_len, D_GROUP), F32),
        vm((nb, LANES, D_GROUP), F32),
        vm((nb, SUBLANES, SSD_CONV_DIM), F32),
        vm((nb, MLSTM_HEADS, MLSTM_DK, MLSTM_DV), F32),
        vm((nb, SUBLANES, MLSTM_DK), F32),
        vm((nb, SUBLANES, LANES), F32),
        vm((nb, 1, D_GROUP), F32),
        vm((nb, SUBLANES, D_GROUP), F32),
        vm((nb, GLA_HEADS // 2, GLA_DV, LANES), F32),
    ]

    kern = functools.partial(_mixer_kernel, nb=nb, t=t, cs=cs, nt=nt, nhalf=nhalf,
                             pre_ln=pre_ln, zero_init=zero_init, pipelined=pipelined,
                             tiles_per_seq=tiles_per_seq, n_steps=n_steps)
    outs = pl.pallas_call(
        kern,
        grid=grid,
        in_specs=in_specs,
        out_specs=out_specs,
        out_shape=out_shape,
        scratch_shapes=scratch,
        compiler_params=pltpu.CompilerParams(
            dimension_semantics=("arbitrary",) * len(grid),
            vmem_limit_bytes=VMEM_LIMIT_BYTES),
        name="mixer_layer",
    )(*args)
    return [outs[0].reshape(bsz, seq, D_MODEL)] + list(outs[1:])


MLP_FF_CHUNK = 1024


def _mlp_kernel(x_ref, w1_ref, b1_ref, w2_ref, b2_ref, g_ref, b_ref, o_ref):
    x = x_ref[...]
    xb = x.astype(BF16)
    acc = jnp.zeros(x.shape, F32)
    for c in range(D_FF // MLP_FF_CHUNK):
        cols = slice(c * MLP_FF_CHUNK, (c + 1) * MLP_FF_CHUNK)
        hid = _dot(xb, w1_ref[:, cols]) + b1_ref[:, cols]
        hid = jnp.square(jnp.maximum(hid, 0.0))
        acc = acc + _dot(hid.astype(BF16), w2_ref[cols, :])
    y = DEEPNORM_ALPHA * x + acc + b2_ref[...]
    o_ref[...] = _layer_norm(y, g_ref[...], b_ref[...])


def _mlp_call(x2d, w1, b1, w2, b2, g, b, *, tm):
    n = x2d.shape[0]
    assert n % tm == 0
    row_spec = pl.BlockSpec((tm, D_MODEL), lambda i: (i, 0))

    def cspec(shape):
        nd = len(shape)
        return pl.BlockSpec(shape, lambda i, _nd=nd: (0,) * _nd, pipeline_mode=pl.Buffered(1))

    return pl.pallas_call(
        _mlp_kernel,
        grid=(n // tm,),
        in_specs=[row_spec] + [cspec(a.shape) for a in (w1, b1, w2, b2, g, b)],
        out_specs=row_spec,
        out_shape=jax.ShapeDtypeStruct(x2d.shape, F32),
        compiler_params=pltpu.CompilerParams(
            dimension_semantics=("arbitrary",), vmem_limit_bytes=VMEM_LIMIT_BYTES),
        name="mlp_layer",
    )(x2d, w1, b1, w2, b2, g, b)


def _pack_w_in(w_in):
    def sl(k):
        return w_in[..., _IN_OFFS[k]:_IN_OFFS[k + 1]]
    (z, xbc, dt, qm, km, vm_, ifm, om, xr, yr, qg, kg, vg, ag, gg) = [sl(k) for k in range(15)]
    pad = jnp.zeros(w_in.shape[:-1] + (LANES - SM_END,), w_in.dtype)
    small = jnp.concatenate([dt, ifm, ag, pad], axis=-1)
    packed = jnp.concatenate([z, xbc, qm, km, vm_, om, xr, yr, qg, kg, vg, gg, small], axis=-1)
    return packed.astype(BF16)


def _lane_row(parts, width=LANES):
    v = jnp.concatenate(parts, axis=-1)
    v = jnp.pad(v, ((0, 0), (0, width - v.shape[-1])))
    return v[:, None, :]


def _block_diag_pairs(wg):
    l = wg.shape[0]
    nblk = RGLRU_BLOCKS // 2
    w5 = wg.reshape(l, 2, nblk, RGLRU_BLOCK_DIM, RGLRU_BLOCK_DIM)
    eye = jnp.eye(nblk, dtype=wg.dtype)
    bd = jnp.einsum('lhnkj,nm->lhnkmj', w5, eye)
    return bd.reshape(l, 2, nblk * RGLRU_BLOCK_DIM, nblk * RGLRU_BLOCK_DIM).astype(BF16)


def _ssd_state_to_kernel(h):
    bsz = h.shape[0]
    ht = jnp.transpose(h, (0, 3, 1, 2)).reshape(bsz, SSD_STATE, D_GROUP)
    hpg = D_GROUP // SSD_GROUPS
    col_group = (jnp.arange(D_GROUP) // hpg)[None, None, :]
    parts = [jnp.where(col_group == g, ht, 0.0) for g in range(SSD_GROUPS)]
    return jnp.concatenate(parts, axis=1)


def _ssd_state_from_kernel(ht):
    bsz = ht.shape[0]
    hpg = D_GROUP // SSD_GROUPS
    h5 = ht.reshape(bsz, SSD_GROUPS, SSD_STATE, SSD_GROUPS, hpg)
    h = jnp.stack([h5[:, g, :, g] for g in range(SSD_GROUPS)], axis=2)
    return jnp.transpose(h.reshape(bsz, SSD_STATE, SSD_HEADS, SSD_HEAD_DIM), (0, 2, 3, 1))


def _gla_state_to_kernel(s):
    bsz = s.shape[0]
    s5 = s.reshape(bsz, GLA_HEADS // 2, 2, GLA_DK, GLA_DV)
    return jnp.transpose(s5, (0, 1, 4, 2, 3)).reshape(bsz, GLA_HEADS // 2, GLA_DV, 2 * GLA_DK)


def _gla_state_from_kernel(st):
    bsz = st.shape[0]
    s5 = st.reshape(bsz, GLA_HEADS // 2, GLA_DV, 2, GLA_DK)
    return jnp.transpose(s5, (0, 1, 3, 4, 2)).reshape(bsz, GLA_HEADS, GLA_DK, GLA_DV)


PROMPT_TILE = 512
MLP_TILE = 512
SAMPLE_SEQS_PER_STEP = 8


def kernel(x_prompt, x_sample, state_ssd_h, state_ssd_conv, state_mlstm_C, state_mlstm_n, state_mlstm_m, state_rglru_h, state_rglru_conv, state_gla_S, ln_in_g, ln_in_b, w_in, ssd_conv_w, ssd_conv_b, ssd_dt_bias, ssd_A_log, ssd_D, ssd_norm_w, mlstm_if_b, mlstm_norm_w, rg_conv_w, rg_conv_b, rg_gate_a_w, rg_gate_a_b, rg_gate_x_w, rg_gate_x_b, rg_lambda, gla_gate_w2, gla_gate_b, gla_norm_w, w_out, ln1_g, ln1_b, mlp_w1, mlp_b1, mlp_w2, mlp_b2, ln2_g, ln2_b):
    nl = w_in.shape[0]
    bp, lp, _ = x_prompt.shape
    bs, ls, _ = x_sample.shape

    def row(a):
        return a[:, None, :]

    w_in_p = _pack_w_in(w_in)
    sb_add = _lane_row([ssd_dt_bias, mlstm_if_b])
    alog = _lane_row([ssd_A_log])
    sd_exp = row(jnp.repeat(ssd_D, SSD_HEAD_DIM, axis=-1))
    rwa = _block_diag_pairs(rg_gate_a_w)
    rwx = _block_diag_pairs(rg_gate_x_w)
    gw2 = jnp.pad(gla_gate_w2, ((0, 0), (SM_AG, LANES - SM_END), (0, 0))).astype(BF16)
    w_out_b = w_out.astype(BF16)
    w1_b = mlp_w1.astype(BF16)
    w2_b = mlp_w2.astype(BF16)
    ln_in = (ln_in_g[None, :], ln_in_b[None, :])

    def layer_weights(l):
        return (w_in_p[l], sb_add[l], alog[l], ssd_conv_w[l], row(ssd_conv_b)[l], sd_exp[l],
                row(ssd_norm_w)[l], row(mlstm_norm_w)[l], rg_conv_w[l], row(rg_conv_b)[l],
                rwa[l], row(rg_gate_a_b)[l], rwx[l], row(rg_gate_x_b)[l], row(rg_lambda)[l],
                gw2[l], row(gla_gate_b)[l], row(gla_norm_w)[l], w_out_b[l], row(ln1_g)[l], row(ln1_b)[l])

    def mlp(x3, l, tm):
        shp = x3.shape
        y = _mlp_call(x3.reshape(-1, D_MODEL), w1_b[l], row(mlp_b1)[l], w2_b[l], row(mlp_b2)[l],
                      row(ln2_g)[l], row(ln2_b)[l], tm=tm)
        return y.reshape(shp)

    def finish_states(st, bsz):
        hT, sconv, c, n, m, rgh, rgconv, gst = st
        return (_ssd_state_from_kernel(hT), sconv, c, n, m[:, :, 0], rgh.reshape(bsz, D_GROUP), rgconv,
                _gla_state_from_kernel(gst))

    xp, xs = x_prompt, x_sample
    p_states, s_states = [], []
    for l in range(nl):
        w = layer_weights(l)
        first = ln_in if l == 0 else None
        outs = _mixer_call(xp, first, None, w, nb=1, t=PROMPT_TILE, cs=CHUNK, pipelined=True)
        xp = mlp(outs[0], l, MLP_TILE)
        p_states.append(finish_states(outs[1:], bp))

        cached = (
            _ssd_state_to_kernel(state_ssd_h[l]), state_ssd_conv[l], state_mlstm_C[l], state_mlstm_n[l],
            jnp.broadcast_to(state_mlstm_m[l][:, :, None], (bs, MLSTM_HEADS, LANES)),
            state_rglru_h[l][:, None, :], state_rglru_conv[l], _gla_state_to_kernel(state_gla_S[l]))
        cs_s = CHUNK if ls % CHUNK == 0 else ls
        outs = _mixer_call(xs, first, cached, w, nb=SAMPLE_SEQS_PER_STEP, t=ls, cs=cs_s, pipelined=False)
        xs = mlp(outs[0], l, bs * ls)
        s_states.append(finish_states(outs[1:], bs))

    p_stacked = [jnp.stack(v, axis=0) for v in zip(*p_states)]
    s_stacked = [jnp.stack(v, axis=0) for v in zip(*s_states)]
    return (xp, xs, *p_stacked, *s_stacked)
```

```python
import functools

import jax
import jax.numpy as jnp
from jax import lax
from jax.experimental import pallas as pl
from jax.experimental.pallas import tpu as pltpu

F32 = jnp.float32
BF16 = jnp.bfloat16

D_MODEL = 1024
DEPTH = 4
CHUNK = 64
CONV_WIDTH = 4
EPS = 1e-5
D_MIX = 2 * D_MODEL
D_GROUP = D_MIX // 4
SSD_HEAD_DIM = 64
SSD_HEADS = D_GROUP // SSD_HEAD_DIM
SSD_GROUPS = 2
SSD_STATE = 64
SSD_CONV_DIM = D_GROUP + 2 * SSD_GROUPS * SSD_STATE
MLSTM_HEADS = 4
MLSTM_DK = D_GROUP // MLSTM_HEADS
MLSTM_DV = D_GROUP // MLSTM_HEADS
RGLRU_BLOCKS = 8
RGLRU_BLOCK_DIM = D_GROUP // RGLRU_BLOCKS
RGLRU_C = 8.0
GLA_HEADS = 4
GLA_DV = D_GROUP // GLA_HEADS
GLA_DK = GLA_DV // 2
GLA_GATE_RANK = 16
GLA_TAU = 16.0
D_FF = 4 * D_MODEL
DEEPNORM_ALPHA = (2 * DEPTH) ** 0.25

IN_SPLIT_SIZES = (
    D_GROUP, SSD_CONV_DIM, SSD_HEADS,
    MLSTM_HEADS * MLSTM_DK, MLSTM_HEADS * MLSTM_DK, MLSTM_HEADS * MLSTM_DV, 2 * MLSTM_HEADS,
    MLSTM_HEADS * MLSTM_DV, D_GROUP, D_GROUP,
    GLA_HEADS * GLA_DK, GLA_HEADS * GLA_DK, GLA_HEADS * GLA_DV, GLA_GATE_RANK, GLA_HEADS * GLA_DV,
)
_IN_OFFS = [0]
for _s in IN_SPLIT_SIZES:
    _IN_OFFS.append(_IN_OFFS[-1] + _s)

LANES = 128
SUBLANES = 8
VMEM_LIMIT_BYTES = 56 * 1024 * 1024

OFF_Z = 0
OFF_XBC = OFF_Z + D_GROUP
OFF_QM = OFF_XBC + SSD_CONV_DIM
OFF_KM = OFF_QM + D_GROUP
OFF_VM = OFF_KM + D_GROUP
OFF_OM = OFF_VM + D_GROUP
OFF_XR = OFF_OM + D_GROUP
OFF_YR = OFF_XR + D_GROUP
OFF_QG = OFF_YR + D_GROUP
OFF_KG = OFF_QG + GLA_HEADS * GLA_DK
OFF_VG = OFF_KG + GLA_HEADS * GLA_DK
OFF_GG = OFF_VG + D_GROUP
OFF_SMALL = OFF_GG + D_GROUP
D_IN_PACKED = OFF_SMALL + LANES
SM_DT = 0
SM_I = SM_DT + SSD_HEADS
SM_F = SM_I + MLSTM_HEADS
SM_AG = SM_F + MLSTM_HEADS
SM_END = SM_AG + GLA_GATE_RANK

MIX_SSD = 0
MIX_ML = D_GROUP
MIX_RG = 2 * D_GROUP
MIX_GLA = 3 * D_GROUP

OUT_PROJ_BLOCKS = 4
RG_BLOCKS_PER_STAGE = 4
CONV_COL_BLOCK = 256


def _dot(a, b):
    return jnp.dot(a, b, preferred_element_type=F32)


def _dot_nt(a, b):
    return lax.dot_general(a, b, (((1,), (1,)), ((), ())), preferred_element_type=F32)


def _dot_tn(a, b):
    return lax.dot_general(a, b, (((0,), (0,)), ((), ())), preferred_element_type=F32)


def _split3(v):
    hi = v.astype(BF16)
    r = v - hi.astype(F32)
    mid = r.astype(BF16)
    lo = (r - mid.astype(F32)).astype(BF16)
    return hi, mid, lo


def _sel_left(m01, v):
    hi, mid, lo = _split3(v)
    return _dot(m01, hi) + _dot(m01, mid) + _dot(m01, lo)


def _sel_right(v, m01):
    hi, mid, lo = _split3(v)
    return _dot(hi, m01) + _dot(mid, m01) + _dot(lo, m01)


def _layer_norm(x, g, b):
    mu = jnp.mean(x, -1, keepdims=True)
    xc = x - mu
    var = jnp.mean(xc * xc, -1, keepdims=True)
    return xc * lax.rsqrt(var + EPS) * g + b


def _chunk_last(v, cs):
    r, c = v.shape
    v3 = v.reshape(r // cs, cs, c)
    last = v3[:, cs - 1:cs, :]
    return jnp.broadcast_to(last, (r // cs, cs, c)).reshape(r, c)


def _run_interleaved(gens):
    live = [[g, 0, n] for g, n in gens]
    while live:
        entry = min(live, key=lambda e: (e[1] + 1) / e[2])
        next(entry[0])
        entry[1] += 1
        if entry[1] == entry[2]:
            assert next(entry[0], live) is live, "generator has more stages than declared"
            live.remove(entry)


def _mixer_kernel(*refs, nb, t, cs, nt, nhalf, pre_ln, zero_init, pipelined, tiles_per_seq, n_steps):
    r = nb * t
    rh = r // nhalf
    it = iter(refs)
    x_ref = next(it)
    if pre_ln:
        lng_ref, lnb_ref = next(it), next(it)
    if not zero_init:
        (i_hT, i_sconv, i_C, i_n, i_m, i_rgh, i_rgconv, i_ST) = [next(it) for _ in range(8)]
    (w_in_ref, sbadd_ref, alog_ref, sconvw_ref, sconvb_ref, sD_ref, snw_ref, mnw_ref,
     rconvw_ref, rconvb_ref, rwa_ref, rba_ref, rwx_ref, rbx_ref, rlam_ref,
     gw2_ref, ggb_ref, gnw_ref, w_out_ref, ln1g_ref, ln1b_ref) = [next(it) for _ in range(21)]
    (o_x, o_hT, o_sconv, o_C, o_n, o_m, o_rgh, o_rgconv, o_ST) = [next(it) for _ in range(9)]
    (xn_ref, xb_ref, sp_ref, cum_ref, z_ref, xsf_ref, xsb_ref, xw_ref, bm_ref, cm_ref, ea_ref,
     q_ref, ks_ref, v_ref, o_ref, qt_ref, kt_ref, kd_ref, dch_ref, vg_ref, gg_ref,
     ra_ref, rb_ref, mix_ref, rgo_ref, xo_ref, sdch_ref, ubs_ref, ubr_ref,
     s_hT, s_sconv, s_C, s_n, s_m, s_rgh, s_rgconv, s_ST) = [next(it) for _ in range(37)]
    tile_bufs = (xn_ref, xb_ref, sp_ref, cum_ref, z_ref, xsf_ref, xsb_ref, xw_ref, bm_ref, cm_ref, ea_ref,
                 q_ref, ks_ref, v_ref, o_ref, qt_ref, kt_ref, kd_ref, dch_ref, vg_ref, gg_ref,
                 ra_ref, rb_ref, mix_ref, rgo_ref, xo_ref, sdch_ref)
    prep_states = (s_sconv, s_rgh, s_rgconv)
    mix_states = (s_hT, s_C, s_n, s_m, s_ST)

    def zero_all(bufs):
        for ref in bufs:
            ref[...] = jnp.zeros(ref.shape, ref.dtype)

    if pipelined:
        step = pl.program_id(0)
        first_tile = 2 * step
        seq_starts = first_tile % tiles_per_seq == 0

        @pl.when(step == 0)
        def _():
            zero_all(tile_bufs + mix_states)

        @pl.when(seq_starts)
        def _():
            zero_all(prep_states)
    else:
        ti = pl.program_id(1)

        @pl.when(ti == 0)
        def _():
            if zero_init:
                zero_all(prep_states + mix_states)
            else:
                s_hT[...] = i_hT[...]
                s_sconv[...] = jnp.zeros_like(s_sconv)
                s_sconv[:, SUBLANES - 3:SUBLANES, :] = i_sconv[...]
                s_C[...] = i_C[...]
                s_n[...] = jnp.zeros_like(s_n)
                s_n[:, 0:MLSTM_HEADS, :] = i_n[...]
                s_m[...] = jnp.zeros_like(s_m)
                s_m[:, 0:MLSTM_HEADS, :] = i_m[...]
                s_rgh[...] = i_rgh[...]
                s_rgconv[...] = jnp.zeros_like(s_rgconv)
                s_rgconv[:, SUBLANES - 3:SUBLANES, :] = i_rgconv[...]
                s_ST[...] = i_ST[...]

    lane = lax.broadcasted_iota(jnp.int32, (1, LANES), 1)
    lane_lo = lane < (LANES // 2)
    head_lane = lane < SM_I
    rr = lax.broadcasted_iota(jnp.int32, (rh, rh), 0)
    cc = lax.broadcasted_iota(jnp.int32, (rh, rh), 1)
    bdtri = jnp.where((rr // cs == cc // cs) & (cc <= rr), 1.0, 0.0).astype(BF16)
    er = lax.broadcasted_iota(jnp.int32, (LANES, D_GROUP), 0)
    ec = lax.broadcasted_iota(jnp.int32, (LANES, D_GROUP), 1)
    expand = jnp.where(ec // SSD_HEAD_DIM == er, 1.0, 0.0).astype(BF16)
    ssd_blockmask = (er // SSD_STATE) == (ec // (D_GROUP // SSD_GROUPS))
    trow = lax.broadcasted_iota(jnp.int32, (cs, cs), 0)
    tcol = lax.broadcasted_iota(jnp.int32, (cs, cs), 1)
    tri = tcol <= trow
    neg_inf = -jnp.inf
    inv_dv = 1.0 / MLSTM_DV

    def segments(hf):
        if nhalf == 1:
            return [(b, b * t, t) for b in range(nb)]
        return [(0, hf * rh, rh)]

    def load_x(hf):
        if nb == 1:
            return x_ref[0, hf * rh:(hf + 1) * rh, :]
        return x_ref[...].reshape(r, D_MODEL)

    def store_out(hf, cols, val):
        if nb == 1:
            o_x[0, hf * rh:(hf + 1) * rh, cols] = val
        else:
            o_x[:, :, cols] = val.reshape(nb, t, val.shape[-1])

    def load_out(hf):
        if nb == 1:
            return o_x[0, hf * rh:(hf + 1) * rh, :]
        return o_x[...].reshape(r, D_MODEL)

    def prep_recurrent(hf):
        h0 = hf * rh
        hr = slice(h0, h0 + rh)
        x = load_x(hf)
        if pre_ln:
            x = _layer_norm(x, lng_ref[...], lnb_ref[...])
        xn_ref[hr, :] = x
        xb_ref[hr, :] = x.astype(BF16)
        yield

        def seg(off, width):
            return _dot(xb_ref[hr, :], w_in_ref[:, off:off + width])

        for cblk in range(SSD_CONV_DIM // CONV_COL_BLOCK):
            cols = slice(cblk * CONV_COL_BLOCK, (cblk + 1) * CONV_COL_BLOCK)
            xbc_raw = seg(OFF_XBC + cblk * CONV_COL_BLOCK, CONV_COL_BLOCK)
            cw = sconvw_ref[:, cols]
            cb = sconvb_ref[:, cols]
            for (b, s0, sl) in segments(hf):
                rows = slice(s0, s0 + sl)
                ubs_ref[0:SUBLANES, cols] = s_sconv[b, :, cols]
                ubs_ref[SUBLANES:SUBLANES + sl, cols] = xbc_raw[s0 - h0:s0 - h0 + sl]
                acc = cb + ubs_ref[5:5 + sl, cols] * cw[0:1]
                for j in range(1, CONV_WIDTH):
                    acc = acc + ubs_ref[5 + j:5 + j + sl, cols] * cw[j:j + 1]
                s_sconv[b, SUBLANES - 3:SUBLANES, cols] = ubs_ref[5 + sl:8 + sl, cols]
                xbc = jax.nn.silu(acc)
                if cblk < D_GROUP // CONV_COL_BLOCK:
                    xsf_ref[rows, cols] = xbc
                    xsb_ref[rows, cols] = xbc.astype(BF16)
                else:
                    bm_ref[rows, :] = xbc[:, 0:LANES].astype(BF16)
                    cm_ref[rows, :] = xbc[:, LANES:2 * LANES].astype(BF16)
            yield

        s_raw = seg(OFF_SMALL, LANES)
        sb = s_raw + sbadd_ref[...]
        dt = jax.nn.softplus(sb)
        logf = jax.nn.log_sigmoid(sb)
        sp = jnp.where(lane < SM_I, dt, jnp.where(lane < SM_F, sb, jnp.where(lane < SM_AG, logf, 0.0)))
        sp_ref[hr, :] = sp
        cum = _sel_left(bdtri, sp)
        a_neg = -jnp.exp(alog_ref[...])
        cuma = jnp.where(head_lane, cum * a_neg, cum)
        cum_ref[hr, :] = cuma
        acum_last = _chunk_last(cuma, cs)
        ea8 = jnp.where(head_lane, jnp.exp(cuma), 0.0)
        wl8 = jnp.where(head_lane, jnp.exp(acum_last - cuma) * sp, 0.0)
        yield
        ea_ref[hr, :] = _dot(ea8.astype(BF16), expand)
        wl_exp = _dot(wl8.astype(BF16), expand)
        dl8 = jnp.where(head_lane, jnp.exp(acum_last), 0.0).reshape(rh // cs, cs, LANES)[:, 0:SUBLANES, :]
        nrow = (rh // cs) * SUBLANES
        sdch_ref[hf * nrow:(hf + 1) * nrow, :] = _sel_right(dl8.reshape(nrow, LANES), expand)
        xw_ref[hr, :] = (xsf_ref[hr, :] * wl_exp).astype(BF16)
        yield

        for cblk in range(D_GROUP // CONV_COL_BLOCK):
            cols = slice(cblk * CONV_COL_BLOCK, (cblk + 1) * CONV_COL_BLOCK)
            xr_raw = seg(OFF_XR + cblk * CONV_COL_BLOCK, CONV_COL_BLOCK)
            rcw = rconvw_ref[:, cols]
            rcb = rconvb_ref[:, cols]
            log_sig_lam = jax.nn.log_sigmoid(rlam_ref[:, cols])
            for (b, s0, sl) in segments(hf):
                rows = slice(s0, s0 + sl)
                ubr_ref[0:SUBLANES, cols] = s_rgconv[b, :, cols]
                ubr_ref[SUBLANES:SUBLANES + sl, cols] = xr_raw[s0 - h0:s0 - h0 + sl]
                xr = rcb + ubr_ref[5:5 + sl, cols] * rcw[0:1]
                for j in range(1, CONV_WIDTH):
                    xr = xr + ubr_ref[5 + j:5 + j + sl, cols] * rcw[j:j + 1]
                s_rgconv[b, SUBLANES - 3:SUBLANES, cols] = ubr_ref[5 + sl:8 + sl, cols]
                xrb = xr.astype(BF16)
                rgate = jax.nn.sigmoid(_dot(xrb, rwa_ref[cblk]) + rba_ref[:, cols])
                igate = jax.nn.sigmoid(_dot(xrb, rwx_ref[cblk]) + rbx_ref[:, cols])
                log_a = RGLRU_C * rgate * log_sig_lam
                a_rg = jnp.exp(log_a)
                ra_ref[rows, cols] = a_rg
                rb_ref[rows, cols] = jnp.sqrt(-jnp.tanh(log_a) * (a_rg * a_rg + 1.0)) * (igate * xr)
            yield
        srow = lax.broadcasted_iota(jnp.int32, (SUBLANES, D_GROUP), 0)
        for (b, s0, sl) in segments(hf):
            h = s_rgh[b]
            for jb in range(sl // SUBLANES):
                rs = slice(s0 + jb * SUBLANES, s0 + (jb + 1) * SUBLANES)
                a = ra_ref[rs, :]
                bb = rb_ref[rs, :]
                for d in (1, 2, 4):
                    a_sh = pltpu.roll(a, d, 0)
                    b_sh = pltpu.roll(bb, d, 0)
                    ok = srow >= d
                    bb = bb + a * jnp.where(ok, b_sh, 0.0)
                    a = a * jnp.where(ok, a_sh, 1.0)
                hb = bb + a * h
                rb_ref[rs, :] = hb
                h = hb[SUBLANES - 1:SUBLANES, :]
                if jb % RG_BLOCKS_PER_STAGE == RG_BLOCKS_PER_STAGE - 1:
                    yield
            s_rgh[b] = h
        yr = seg(OFF_YR, D_GROUP)
        rgo_ref[hr, :] = (rb_ref[hr, :] * jax.nn.gelu(yr)).astype(BF16)
        yield

    def prep_project(hf):
        hr = slice(hf * rh, (hf + 1) * rh)

        def seg(off, width):
            return _dot(xb_ref[hr, :], w_in_ref[:, off:off + width])

        z_ref[hr, :] = seg(OFF_Z, D_GROUP)
        yield
        vg_ref[hr, :] = seg(OFF_VG, D_GROUP).astype(BF16)
        yield
        s_raw = seg(OFF_SMALL, LANES)
        la = jax.nn.log_sigmoid(_dot(s_raw.astype(BF16), gw2_ref[...]) + ggb_ref[...]) * (1.0 / GLA_TAU)
        bcum = _sel_left(bdtri, la)
        bl = _chunk_last(bcum, cs)
        dch_ref[hr, :] = jnp.exp(bl)
        yield
        gg_ref[hr, :] = seg(OFF_GG, D_GROUP)
        yield
        qg = seg(OFF_QG, GLA_HEADS * GLA_DK) * (GLA_DK ** -0.5)
        qt_ref[hr, :] = (qg * jnp.exp(bcum)).astype(BF16)
        kg = seg(OFF_KG, GLA_HEADS * GLA_DK)
        kt_ref[hr, :] = (kg * jnp.exp(-bcum)).astype(BF16)
        kd_ref[hr, :] = (kg * jnp.exp(bl - bcum)).astype(BF16)
        yield
        q_ref[hr, :] = seg(OFF_QM, D_GROUP).astype(BF16)
        yield
        ks_ref[hr, :] = seg(OFF_KM, D_GROUP) * (MLSTM_DK ** -0.5)
        yield
        v_ref[hr, :] = seg(OFF_VM, D_GROUP).astype(BF16)
        yield
        o_ref[hr, :] = seg(OFF_OM, D_GROUP)
        yield

    def chunk_plan(hf):
        ids = list(range(hf * rh // cs, (hf + 1) * rh // cs))
        rows_of = {i: slice(i * cs, (i + 1) * cs) for i in ids}
        last_rows_of = {i: slice((i + 1) * cs - SUBLANES, (i + 1) * cs) for i in ids}
        seq_of = {i: (i * cs) // t for i in ids}
        load_at = {i: (i == ids[0]) or ((i * cs) % t == 0) for i in ids}
        store_at = {i: (i == ids[-1]) or (((i + 1) * cs) % t == 0) for i in ids}
        return ids, rows_of, last_rows_of, seq_of, load_at, store_at

    def mix_gens(hf):
        ids, rows_of, last_rows_of, seq_of, load_at, store_at = chunk_plan(hf)
        shared = {}

        def transposed():
            if not shared:
                shared['sp_c'] = {i: sp_ref[rows_of[i], :] for i in ids}
                shared['cum_c'] = {i: cum_ref[rows_of[i], :] for i in ids}
                shared['sp_t'] = {i: shared['sp_c'][i].T for i in ids}
                shared['cum_t'] = {i: shared['cum_c'][i].T for i in ids}
            return shared['sp_c'], shared['cum_c'], shared['sp_t'], shared['cum_t']

        def ssd():
            sp_c, cum_c, sp_t, cum_t = transposed()
            cm_c = {i: cm_ref[rows_of[i], :] for i in ids}
            bm_c = {i: bm_ref[rows_of[i], :] for i in ids}
            zero_b = jnp.zeros((cs, LANES), BF16)
            g_mat = {i: [_dot_nt(jnp.where(lane_lo, cm_c[i], zero_b), bm_c[i]),
                         _dot_nt(jnp.where(lane_lo, zero_b, cm_c[i]), bm_c[i])] for i in ids}
            upd = {i: _dot_tn(bm_c[i], xw_ref[rows_of[i], :]) for i in ids}
            yield
            m_ssd = {}
            for i in ids:
                per_head = []
                for h in range(SSD_HEADS):
                    col = cum_c[i][:, SM_DT + h:SM_DT + h + 1]
                    row = cum_t[i][SM_DT + h:SM_DT + h + 1, :]
                    dtrow = sp_t[i][SM_DT + h:SM_DT + h + 1, :]
                    decay = jnp.exp(jnp.where(tri, col - row, neg_inf))
                    per_head.append((g_mat[i][h // (SSD_HEADS // SSD_GROUPS)] * decay * dtrow).astype(BF16))
                m_ssd[i] = per_head
                if i % 2 == 1:
                    yield
            y_intra = {}
            for i in ids:
                xs_b = xsb_ref[rows_of[i], :]
                pairs = []
                for hp in range(SSD_HEADS // 2):
                    blk = xs_b[:, hp * LANES:(hp + 1) * LANES]
                    pairs.append(jnp.where(lane_lo, _dot(m_ssd[i][2 * hp], blk), _dot(m_ssd[i][2 * hp + 1], blk)))
                y_intra[i] = jnp.concatenate(pairs, axis=1)
            yield
            h_in = {}
            h_t = None
            for i in ids:
                if load_at[i]:
                    h_t = s_hT[seq_of[i]]
                h_in[i] = h_t
                dch = sdch_ref[i * SUBLANES:i * SUBLANES + 1, :]
                h_t = dch * h_t + jnp.where(ssd_blockmask, upd[i], 0.0)
                if store_at[i]:
                    s_hT[seq_of[i]] = h_t
            yield
            for i in ids:
                rs = rows_of[i]
                y = _dot(cm_c[i], h_in[i].astype(BF16)) * ea_ref[rs, :]
                y = y + y_intra[i] + sD_ref[...] * xsf_ref[rs, :]
                yz = y * jax.nn.silu(z_ref[rs, :])
                yz = yz * lax.rsqrt(jnp.mean(yz * yz, -1, keepdims=True) + EPS) * snw_ref[...]
                mix_ref[rs, MIX_SSD:MIX_SSD + D_GROUP] = yz.astype(BF16)
                if i % 2 == 1:
                    yield

        def mlstm(h):
            sp_c, cum_c, sp_t, cum_t = transposed()
            hl = slice(h * MLSTM_DK, (h + 1) * MLSTM_DK)
            q_h = {i: q_ref[rows_of[i], hl] for i in ids}
            v_h = {i: v_ref[rows_of[i], hl] for i in ids}
            k_h = {i: ks_ref[rows_of[i], hl] for i in ids}
            bcol = {i: cum_c[i][:, SM_F + h:SM_F + h + 1] for i in ids}
            dmat, m_intra, b_last, m_loc, kw = {}, {}, {}, {}, {}
            for i in ids:
                brow = cum_t[i][SM_F + h:SM_F + h + 1, :]
                icol = sp_c[i][:, SM_I + h:SM_I + h + 1]
                irow = sp_t[i][SM_I + h:SM_I + h + 1, :]
                dm = jnp.where(tri, bcol[i] - brow + irow, neg_inf)
                dmat[i] = dm
                m_intra[i] = jnp.max(dm, axis=1, keepdims=True)
                bl1 = bcol[i][cs - 1:cs, :]
                b_last[i] = bl1
                lw = bl1 - bcol[i] + icol
                ml = jnp.max(lw, axis=0, keepdims=True)
                m_loc[i] = ml
                kw[i] = k_h[i] * jnp.exp(lw - ml)
            qk_raw = {i: _dot_nt(q_h[i], k_h[i].astype(BF16)) for i in ids}
            c_chunk = {i: _dot_tn(kw[i].astype(BF16), v_h[i]) for i in ids}
            n_chunk = {i: jnp.sum(kw[i], axis=0, keepdims=True) for i in ids}
            yield
            qk = {i: qk_raw[i] * jnp.exp(dmat[i] - m_intra[i]) for i in ids}
            num_intra = {i: _dot(qk[i].astype(BF16), v_h[i]) for i in ids}
            den_intra = {i: jnp.sum(qk[i], axis=1, keepdims=True) for i in ids}
            yield
            c_in, n_in, m_in = {}, {}, {}
            c_s = n_s = m_s = None
            for i in ids:
                sq = seq_of[i]
                if load_at[i]:
                    c_s = s_C[sq, h]
                    n_s = s_n[sq, h:h + 1, :]
                    m_s = s_m[sq, h:h + 1, :][:, 0:1]
                c_in[i], n_in[i], m_in[i] = c_s, n_s, m_s
                m_new = jnp.maximum(b_last[i] + m_s, m_loc[i])
                a_old = jnp.exp(b_last[i] + m_s - m_new)
                a_new = jnp.exp(m_loc[i] - m_new)
                c_s = a_old * c_s + a_new * c_chunk[i]
                n_s = a_old * n_s + a_new * n_chunk[i]
                m_s = m_new
                if store_at[i]:
                    s_C[sq, h] = c_s
                    s_n[sq, h:h + 1, :] = n_s
                    s_m[sq, h:h + 1, :] = jnp.broadcast_to(m_s, (1, LANES))
            yield
            inter = {i: _dot(q_h[i], c_in[i].astype(BF16)) for i in ids}
            qn = {i: jnp.sum(q_h[i].astype(F32) * n_in[i], axis=1, keepdims=True) for i in ids}
            yield
            for i in ids:
                rs = rows_of[i]
                g = bcol[i] + m_in[i]
                m_t = jnp.maximum(g, m_intra[i])
                a_inter = jnp.exp(g - m_t)
                a_intra = jnp.exp(m_intra[i] - m_t)
                num = a_inter * inter[i] + a_intra * num_intra[i]
                den = a_inter * qn[i] + a_intra * den_intra[i]
                hout = num / jnp.maximum(jnp.abs(den), jnp.exp(-m_t))
                hn = hout * lax.rsqrt(jnp.sum(hout * hout, -1, keepdims=True) * inv_dv + EPS)
                ym = jax.nn.sigmoid(o_ref[rs, hl]) * (hn * mnw_ref[:, hl])
                mix_ref[rs, MIX_ML + h * MLSTM_DV:MIX_ML + (h + 1) * MLSTM_DV] = ym.astype(BF16)
            yield

        def gla(j):
            bl_ = slice(j * LANES, (j + 1) * LANES)
            zq = jnp.zeros((cs, LANES), BF16)
            qt = {i: qt_ref[rows_of[i], bl_] for i in ids}
            kt = {i: kt_ref[rows_of[i], bl_] for i in ids}
            kd = {i: kd_ref[rows_of[i], bl_] for i in ids}
            qm = {i: [jnp.where(lane_lo, qt[i], zq), jnp.where(lane_lo, zq, qt[i])] for i in ids}
            v_h = {i: [vg_ref[rows_of[i], (2 * j + h2) * GLA_DV:(2 * j + h2 + 1) * GLA_DV] for h2 in range(2)]
                   for i in ids}
            att_raw = {i: [_dot_nt(qm[i][h2], kt[i]) for h2 in range(2)] for i in ids}
            prods = {i: [_dot_tn(v_h[i][h2], kd[i]) for h2 in range(2)] for i in ids}
            yield
            att = {i: [jnp.where(tri, att_raw[i][h2], 0.0).astype(BF16) for h2 in range(2)] for i in ids}
            o_intra = {i: [_dot(att[i][h2], v_h[i][h2]) for h2 in range(2)] for i in ids}
            yield
            st_in = {}
            st = None
            for i in ids:
                sq = seq_of[i]
                if load_at[i]:
                    st = s_ST[sq, j]
                st_in[i] = st
                dg = dch_ref[last_rows_of[i], bl_][SUBLANES - 1:SUBLANES, :]
                st = dg * st + jnp.where(lane_lo, prods[i][0], prods[i][1])
                if store_at[i]:
                    s_ST[sq, j] = st
            yield
            o_inter = {i: [_dot_nt(qm[i][h2], st_in[i].astype(BF16)) for h2 in range(2)] for i in ids}
            yield
            for i in ids:
                rs = rows_of[i]
                for h2 in range(2):
                    h = 2 * j + h2
                    hv = slice(h * GLA_DV, (h + 1) * GLA_DV)
                    o = o_intra[i][h2] + o_inter[i][h2]
                    on = o * lax.rsqrt(jnp.sum(o * o, -1, keepdims=True) * (1.0 / GLA_DV) + EPS)
                    yg = on * gnw_ref[:, hv] * jax.nn.silu(gg_ref[rs, hv])
                    mix_ref[rs, MIX_GLA + h * GLA_DV:MIX_GLA + (h + 1) * GLA_DV] = yg.astype(BF16)
            yield

        def handoff():
            hr = slice(hf * rh, (hf + 1) * rh)
            mix_ref[hr, MIX_RG:MIX_RG + D_GROUP] = rgo_ref[hr, :]
            yield
            xo_ref[hr, :] = xn_ref[hr, :]
            yield

        n_odd = sum(1 for i in ids if i % 2 == 1)
        return ([(ssd(), 3 + 2 * n_odd)] + [(mlstm(h), 5) for h in range(MLSTM_HEADS)]
                + [(gla(j), 5) for j in range(GLA_HEADS // 2)] + [(handoff(), 2)])

    def prep_gens(hf):
        n_scan = sum((sl // SUBLANES) // RG_BLOCKS_PER_STAGE for (_, _, sl) in segments(hf))
        n_recurrent = 1 + SSD_CONV_DIM // CONV_COL_BLOCK + 2 + D_GROUP // CONV_COL_BLOCK + n_scan + 1
        return [(prep_recurrent(hf), n_recurrent), (prep_project(hf), 9)]

    out_stages = OUT_PROJ_BLOCKS + 1

    def outp(hf):
        hr = slice(hf * rh, (hf + 1) * rh)
        wblk = D_MODEL // OUT_PROJ_BLOCKS
        for c in range(OUT_PROJ_BLOCKS):
            cols = slice(c * wblk, (c + 1) * wblk)
            y1 = DEEPNORM_ALPHA * xo_ref[hr, cols] + _dot(mix_ref[hr, :], w_out_ref[:, cols])
            store_out(hf, cols, y1)
            yield
        store_out(hf, slice(0, D_MODEL), _layer_norm(load_out(hf), ln1g_ref[...], ln1b_ref[...]))
        yield

    def write_mix_states():
        o_hT[...] = s_hT[...]
        o_C[...] = s_C[...]
        o_n[...] = s_n[:, 0:MLSTM_HEADS, :]
        o_m[...] = s_m[:, 0:MLSTM_HEADS, :]
        o_ST[...] = s_ST[...]

    def write_prep_states():
        o_sconv[...] = s_sconv[:, SUBLANES - 3:SUBLANES, :]
        o_rgh[...] = s_rgh[...]
        o_rgconv[...] = s_rgconv[:, SUBLANES - 3:SUBLANES, :]

    if pipelined:
        def phase(prep_half, mix_half):
            _run_interleaved(prep_gens(prep_half) + mix_gens(mix_half) + [(outp(prep_half), out_stages)])

        phase(0, 1)

        @pl.when(seq_starts & (step > 0))
        def _():
            write_mix_states()

        @pl.when(seq_starts)
        def _():
            zero_all(mix_states)

        phase(1, 0)

        @pl.when(((first_tile + 2) % tiles_per_seq == 0) & (step < n_steps - 1))
        def _():
            write_prep_states()
    else:
        _run_interleaved(prep_gens(0))
        _run_interleaved(mix_gens(0))
        _run_interleaved([(outp(0), out_stages)])

        @pl.when(ti == nt - 1)
        def _():
            write_mix_states()
            write_prep_states()


def _layer_spec(shape, layer):
    nd = len(shape)
    return pl.BlockSpec((None,) + tuple(shape[1:]), lambda *_, _nd=nd: (layer,) + (0,) * (_nd - 1),
                        pipeline_mode=pl.Buffered(1))


def _mixer_call(x, ln_in, states, w, layer, *, nb, t, cs, pipelined):
    bsz, seq, _ = x.shape
    assert bsz % nb == 0 and seq % t == 0 and t % SUBLANES == 0
    nhalf = 2 if pipelined else 1
    assert not pipelined or (nb == 1 and states is None)
    nt = seq // t
    r = nb * t
    assert r % nhalf == 0 and (r // nhalf) % cs == 0 and t % cs == 0
    seg_len = t if nhalf == 1 else r // nhalf
    pre_ln = ln_in is not None
    zero_init = states is None
    tiles_per_seq = nhalf * nt
    n_blocks = bsz * nt
    n_steps = n_blocks + 1 if pipelined else None

    state_shapes = (
        (bsz, LANES, D_GROUP),
        (bsz, CONV_WIDTH - 1, SSD_CONV_DIM),
        (bsz, MLSTM_HEADS, MLSTM_DK, MLSTM_DV),
        (bsz, MLSTM_HEADS, MLSTM_DK),
        (bsz, MLSTM_HEADS, LANES),
        (bsz, 1, D_GROUP),
        (bsz, CONV_WIDTH - 1, D_GROUP),
        (bsz, GLA_HEADS // 2, GLA_DV, LANES),
    )

    prep_state_ids = (1, 5, 6)

    if pipelined:
        n_tiles = nhalf * n_blocks

        def state_spec(shape, idx):
            nd = len(shape)
            if idx in prep_state_ids:
                return pl.BlockSpec((1,) + shape[1:], lambda s, _nd=nd: (
                    jnp.minimum(2 * s + 1, n_tiles - 1) // tiles_per_seq,) + (0,) * (_nd - 1))
            return pl.BlockSpec((1,) + shape[1:], lambda s, _nd=nd: (
                jnp.maximum(2 * s - 1, 0) // tiles_per_seq,) + (0,) * (_nd - 1))

        x = x.reshape(n_blocks, t, D_MODEL)
        x_spec = pl.BlockSpec((1, t, D_MODEL), lambda s: (jnp.minimum(s, n_blocks - 1), 0, 0))
        out_x_spec = pl.BlockSpec((1, t, D_MODEL), lambda s: (jnp.maximum(s - 1, 0), 0, 0))
        grid = (n_steps,)
    else:
        def state_spec(shape, idx):
            blk = (nb,) + shape[1:]
            nd = len(shape)
            return pl.BlockSpec(blk, lambda b, s, _nd=nd: (b,) + (0,) * (_nd - 1))

        x_spec = pl.BlockSpec((nb, t, D_MODEL), lambda b, s: (b, s, 0))
        out_x_spec = x_spec
        grid = (bsz // nb, nt)
    args = [x]
    in_specs = [x_spec]
    if pre_ln:
        args += list(ln_in)
        in_specs += [_layer_spec(a.shape, 0) for a in ln_in]
    if not zero_init:
        args += list(states)
        for s in state_shapes:
            nd = len(s)
            in_specs.append(pl.BlockSpec((None, nb) + s[1:],
                                         lambda b, t_, _nd=nd: (layer, b) + (0,) * (_nd - 1)))
    args += list(w)
    in_specs += [_layer_spec(a.shape, layer) for a in w]

    out_shape = [jax.ShapeDtypeStruct(x.shape, F32)] + [jax.ShapeDtypeStruct(s, F32) for s in state_shapes]
    out_specs = [out_x_spec] + [state_spec(s, i) for i, s in enumerate(state_shapes)]

    vm = pltpu.VMEM
    scratch = [
        vm((r, D_MODEL), F32),
        vm((r, D_MODEL), BF16),
        vm((r, LANES), F32),
        vm((r, LANES), F32),
        vm((r, D_GROUP), F32),
        vm((r, D_GROUP), F32),
        vm((r, D_GROUP), BF16),
        vm((r, D_GROUP), BF16),
        vm((r, LANES), BF16),
        vm((r, LANES), BF16),
        vm((r, D_GROUP), F32),
        vm((r, D_GROUP), BF16),
        vm((r, D_GROUP), F32),
        vm((r, D_GROUP), BF16),
        vm((r, D_GROUP), F32),
        vm((r, GLA_HEADS * GLA_DK), BF16),
        vm((r, GLA_HEADS * GLA_DK), BF16),
        vm((r, GLA_HEADS * GLA_DK), BF16),
        vm((r, GLA_HEADS * GLA_DK), F32),
        vm((r, D_GROUP), BF16),
        vm((r, D_GROUP), F32),
        vm((r, D_GROUP), F32),
        vm((r, D_GROUP), F32),
        vm((r, D_MIX), BF16),
        vm((r, D_GROUP), BF16),
        vm((r, D_MODEL), F32),
        vm((r // cs * SUBLANES, D_GROUP), F32),
        vm((SUBLANES + seg_len, SSD_CONV_DIM), F32),
        vm((SUBLANES + seg_len, D_GROUP), F32),
        vm((nb, LANES, D_GROUP), F32),
        vm((nb, SUBLANES, SSD_CONV_DIM), F32),
        vm((nb, MLSTM_HEADS, MLSTM_DK, MLSTM_DV), F32),
        vm((nb, SUBLANES, MLSTM_DK), F32),
        vm((nb, SUBLANES, LANES), F32),
        vm((nb, 1, D_GROUP), F32),
        vm((nb, SUBLANES, D_GROUP), F32),
        vm((nb, GLA_HEADS // 2, GLA_DV, LANES), F32),
    ]

    kern = functools.partial(_mixer_kernel, nb=nb, t=t, cs=cs, nt=nt, nhalf=nhalf,
                             pre_ln=pre_ln, zero_init=zero_init, pipelined=pipelined,
                             tiles_per_seq=tiles_per_seq, n_steps=n_steps)
    outs = pl.pallas_call(
        kern,
        grid=grid,
        in_specs=in_specs,
        out_specs=out_specs,
        out_shape=out_shape,
        scratch_shapes=scratch,
        compiler_params=pltpu.CompilerParams(
            dimension_semantics=("arbitrary",) * len(grid),
            vmem_limit_bytes=VMEM_LIMIT_BYTES),
        name="mixer_layer",
    )(*args)
    return [outs[0].reshape(bsz, seq, D_MODEL)] + list(outs[1:])


MLP_FF_CHUNK = 1024


def _mlp_kernel(x_ref, w1_ref, b1_ref, w2_ref, b2_ref, g_ref, b_ref, o_ref):
    x = x_ref[...]
    xb = x.astype(BF16)
    acc = jnp.zeros(x.shape, F32)
    for c in range(D_FF // MLP_FF_CHUNK):
        cols = slice(c * MLP_FF_CHUNK, (c + 1) * MLP_FF_CHUNK)
        hid = _dot(xb, w1_ref[:, cols]) + b1_ref[:, cols]
        hid = jnp.square(jnp.maximum(hid, 0.0))
        acc = acc + _dot(hid.astype(BF16), w2_ref[cols, :])
    y = DEEPNORM_ALPHA * x + acc + b2_ref[...]
    o_ref[...] = _layer_norm(y, g_ref[...], b_ref[...])


def _mlp_call(x2d, w1, b1, w2, b2, g, b, layer, *, tm):
    n = x2d.shape[0]
    assert n % tm == 0
    row_spec = pl.BlockSpec((tm, D_MODEL), lambda i: (i, 0))

    return pl.pallas_call(
        _mlp_kernel,
        grid=(n // tm,),
        in_specs=[row_spec] + [_layer_spec(a.shape, layer) for a in (w1, b1, w2, b2, g, b)],
        out_specs=row_spec,
        out_shape=jax.ShapeDtypeStruct(x2d.shape, F32),
        compiler_params=pltpu.CompilerParams(
            dimension_semantics=("arbitrary",), vmem_limit_bytes=VMEM_LIMIT_BYTES),
        name="mlp_layer",
    )(x2d, w1, b1, w2, b2, g, b)


def _pack_w_in(w_in):
    def sl(k):
        return w_in[..., _IN_OFFS[k]:_IN_OFFS[k + 1]]
    (z, xbc, dt, qm, km, vm_, ifm, om, xr, yr, qg, kg, vg, ag, gg) = [sl(k) for k in range(15)]
    pad = jnp.zeros(w_in.shape[:-1] + (LANES - SM_END,), w_in.dtype)
    small = jnp.concatenate([dt, ifm, ag, pad], axis=-1)
    packed = jnp.concatenate([z, xbc, qm, km, vm_, om, xr, yr, qg, kg, vg, gg, small], axis=-1)
    return packed.astype(BF16)


def _lane_row(parts, width=LANES):
    v = jnp.concatenate(parts, axis=-1)
    v = jnp.pad(v, ((0, 0), (0, width - v.shape[-1])))
    return v[:, None, :]


def _block_diag_pairs(wg):
    l = wg.shape[0]
    nblk = RGLRU_BLOCKS // 2
    w5 = wg.reshape(l, 2, nblk, RGLRU_BLOCK_DIM, RGLRU_BLOCK_DIM)
    eye = jnp.eye(nblk, dtype=wg.dtype)
    bd = jnp.einsum('lhnkj,nm->lhnkmj', w5, eye)
    return bd.reshape(l, 2, nblk * RGLRU_BLOCK_DIM, nblk * RGLRU_BLOCK_DIM).astype(BF16)


def _ssd_state_to_kernel(h):
    bsz = h.shape[0]
    ht = jnp.transpose(h, (0, 3, 1, 2)).reshape(bsz, SSD_STATE, D_GROUP)
    hpg = D_GROUP // SSD_GROUPS
    col_group = (jnp.arange(D_GROUP) // hpg)[None, None, :]
    parts = [jnp.where(col_group == g, ht, 0.0) for g in range(SSD_GROUPS)]
    return jnp.concatenate(parts, axis=1)


def _ssd_state_from_kernel(ht):
    bsz = ht.shape[0]
    hpg = D_GROUP // SSD_GROUPS
    h5 = ht.reshape(bsz, SSD_GROUPS, SSD_STATE, SSD_GROUPS, hpg)
    h = jnp.stack([h5[:, g, :, g] for g in range(SSD_GROUPS)], axis=2)
    return jnp.transpose(h.reshape(bsz, SSD_STATE, SSD_HEADS, SSD_HEAD_DIM), (0, 2, 3, 1))


def _gla_state_to_kernel(s):
    bsz = s.shape[0]
    s5 = s.reshape(bsz, GLA_HEADS // 2, 2, GLA_DK, GLA_DV)
    return jnp.transpose(s5, (0, 1, 4, 2, 3)).reshape(bsz, GLA_HEADS // 2, GLA_DV, 2 * GLA_DK)


def _gla_state_from_kernel(st):
    bsz = st.shape[0]
    s5 = st.reshape(bsz, GLA_HEADS // 2, GLA_DV, 2, GLA_DK)
    return jnp.transpose(s5, (0, 1, 3, 4, 2)).reshape(bsz, GLA_HEADS, GLA_DK, GLA_DV)


PROMPT_TILE = 512
MLP_TILE = 512
SAMPLE_SEQS_PER_STEP = 8


def kernel(x_prompt, x_sample, state_ssd_h, state_ssd_conv, state_mlstm_C, state_mlstm_n, state_mlstm_m, state_rglru_h, state_rglru_conv, state_gla_S, ln_in_g, ln_in_b, w_in, ssd_conv_w, ssd_conv_b, ssd_dt_bias, ssd_A_log, ssd_D, ssd_norm_w, mlstm_if_b, mlstm_norm_w, rg_conv_w, rg_conv_b, rg_gate_a_w, rg_gate_a_b, rg_gate_x_w, rg_gate_x_b, rg_lambda, gla_gate_w2, gla_gate_b, gla_norm_w, w_out, ln1_g, ln1_b, mlp_w1, mlp_b1, mlp_w2, mlp_b2, ln2_g, ln2_b):
    nl = w_in.shape[0]
    bp, lp, _ = x_prompt.shape
    bs, ls, _ = x_sample.shape

    def row(a):
        return a[:, None, :]

    w_in_p = _pack_w_in(w_in)
    sb_add = _lane_row([ssd_dt_bias, mlstm_if_b])
    alog = _lane_row([ssd_A_log])
    sd_exp = row(jnp.repeat(ssd_D, SSD_HEAD_DIM, axis=-1))
    rwa = _block_diag_pairs(rg_gate_a_w)
    rwx = _block_diag_pairs(rg_gate_x_w)
    gw2 = jnp.pad(gla_gate_w2, ((0, 0), (SM_AG, LANES - SM_END), (0, 0))).astype(BF16)
    w_out_b = w_out.astype(BF16)
    w1_b = mlp_w1.astype(BF16)
    w2_b = mlp_w2.astype(BF16)
    ln_in = (ln_in_g[None, None, :], ln_in_b[None, None, :])

    mixer_w = (w_in_p, sb_add, alog, ssd_conv_w, row(ssd_conv_b), sd_exp, row(ssd_norm_w), row(mlstm_norm_w),
               rg_conv_w, row(rg_conv_b), rwa, row(rg_gate_a_b), rwx, row(rg_gate_x_b), row(rg_lambda),
               gw2, row(gla_gate_b), row(gla_norm_w), w_out_b, row(ln1_g), row(ln1_b))
    mlp_w = (w1_b, row(mlp_b1), w2_b, row(mlp_b2), row(ln2_g), row(ln2_b))

    def mlp(x3, l, tm):
        shp = x3.shape
        return _mlp_call(x3.reshape(-1, D_MODEL), *mlp_w, l, tm=tm).reshape(shp)

    def finish_states(per_layer, bsz):
        hT, sconv, c, n, m, rgh, rgconv, gst = [jnp.stack(v, axis=0) for v in zip(*per_layer)]
        lead = (nl, bsz)

        def flat(a):
            return a.reshape((nl * bsz,) + a.shape[2:])

        h = _ssd_state_from_kernel(flat(hT))
        s = _gla_state_from_kernel(flat(gst))
        return (h.reshape(lead + h.shape[1:]), sconv, c, n, m[..., 0], rgh.reshape(nl, bsz, D_GROUP), rgconv,
                s.reshape(lead + s.shape[1:]))

    def flat_layers(a):
        return a.reshape((nl * bs,) + a.shape[2:])

    cached = (
        _ssd_state_to_kernel(flat_layers(state_ssd_h)).reshape(nl, bs, LANES, D_GROUP),
        state_ssd_conv, state_mlstm_C, state_mlstm_n,
        jnp.broadcast_to(state_mlstm_m[..., None], (nl, bs, MLSTM_HEADS, LANES)),
        state_rglru_h[:, :, None, :], state_rglru_conv,
        _gla_state_to_kernel(flat_layers(state_gla_S)).reshape(nl, bs, GLA_HEADS // 2, GLA_DV, LANES))
    cs_s = CHUNK if ls % CHUNK == 0 else ls

    xp, xs = x_prompt, x_sample
    p_states, s_states = [], []
    for l in range(nl):
        first = ln_in if l == 0 else None
        outs = _mixer_call(xp, first, None, mixer_w, l, nb=1, t=PROMPT_TILE, cs=CHUNK, pipelined=True)
        xp = mlp(outs[0], l, MLP_TILE)
        p_states.append(outs[1:])
        outs = _mixer_call(xs, first, cached, mixer_w, l, nb=SAMPLE_SEQS_PER_STEP, t=ls, cs=cs_s,
                           pipelined=False)
        xs = mlp(outs[0], l, bs * ls)
        s_states.append(outs[1:])

    return (xp, xs, *finish_states(p_states, bp), *finish_states(s_states, bs))
```
